```python
import jax, jax.numpy as jnp
from jax import lax
import numpy as np

D_MODEL = 1024
BATCH = 4
SEQ = 8192
DEPTH = 2

SC_WIDTH = D_MODEL // 2
SC_KERNEL = 3
MLA_HEADS = 8
MLA_NOPE = 64
MLA_ROPE = 32
MLA_V = 64
MLA_Q_RANK = 384
MLA_KV_RANK = 256
ROPE_BASE = 10000.0
Q_BLOCK = 128
IN_COLS = 3 * SC_WIDTH + MLA_Q_RANK + MLA_KV_RANK + MLA_ROPE
MIX_WIDTH = SC_WIDTH + MLA_HEADS * MLA_V
CONF_KERNEL = 31
CONF_WIDTH = D_MODEL
D_FF = 4 * D_MODEL
N_EVEN = (DEPTH + 1) // 2
N_ODD = DEPTH // 2
EPS = 1e-6

kernel_name = "hybrid_shortconv_mla_conformer_encoder"


def rms_norm(x, g):
    xf = x.astype(jnp.float32)
    y = xf * lax.rsqrt(jnp.mean(xf * xf, axis=-1, keepdims=True) + EPS)
    return (y * g.astype(jnp.float32)).astype(x.dtype)


def layer_norm(x, g, b):
    xf = x.astype(jnp.float32)
    mu = jnp.mean(xf, axis=-1, keepdims=True)
    var = jnp.mean(jnp.square(xf - mu), axis=-1, keepdims=True)
    y = (xf - mu) * lax.rsqrt(var + EPS)
    return (y * g.astype(jnp.float32) + b.astype(jnp.float32)).astype(x.dtype)


def depthwise_conv(x, w):
    k = w.shape[0]
    return lax.conv_general_dilated(
        x, w[:, None, :].astype(x.dtype), window_strides=(1,),
        padding=[(k // 2, k // 2)], dimension_numbers=('NWC', 'WIO', 'NWC'),
        feature_group_count=x.shape[-1])


def apply_rope(x, cos, sin):
    half = x.shape[-1] // 2
    xf = x.astype(jnp.float32)
    x1, x2 = xf[..., :half], xf[..., half:]
    out = jnp.concatenate([x1 * cos - x2 * sin, x2 * cos + x1 * sin], axis=-1)
    return out.astype(x.dtype)


def block_attention(q, k, v):
    b, s, h, dq = q.shape
    nblk = s // Q_BLOCK
    qb = q.reshape(b, nblk, Q_BLOCK, h, dq).transpose(1, 0, 2, 3, 4)
    kf = k.astype(jnp.float32)
    scale = dq ** -0.5

    def one_block(q_blk):
        sc = jnp.einsum('bqhd,bkhd->bhqk', q_blk.astype(jnp.float32), kf) * scale
        p = jax.nn.softmax(sc, axis=-1)
        return jnp.einsum('bhqk,bkhd->bqhd', p.astype(v.dtype), v)

    o = lax.map(one_block, qb)
    return o.transpose(1, 0, 2, 3, 4).reshape(b, s, h * v.shape[-1])


def parallel_conv_mla(h, positions, w_in, sc_kernel, q_norm, w_uq, kv_norm, w_ukv, w_out):
    b, s, _ = h.shape
    proj = h @ w_in
    cuts = [SC_WIDTH, 2 * SC_WIDTH, 3 * SC_WIDTH,
            3 * SC_WIDTH + MLA_Q_RANK, 3 * SC_WIDTH + MLA_Q_RANK + MLA_KV_RANK]
    gate_b, gate_c, xs, q_lat, kv_lat, k_rope = jnp.split(proj, cuts, axis=-1)

    y_a = gate_b * depthwise_conv(gate_c * xs, sc_kernel)

    q = (rms_norm(q_lat, q_norm) @ w_uq).reshape(b, s, MLA_HEADS, MLA_NOPE + MLA_ROPE)
    q_nope, q_pe = q[..., :MLA_NOPE], q[..., MLA_NOPE:]
    kv = (rms_norm(kv_lat, kv_norm) @ w_ukv).reshape(b, s, MLA_HEADS, MLA_NOPE + MLA_V)
    k_nope, v = kv[..., :MLA_NOPE], kv[..., MLA_NOPE:]
    half = MLA_ROPE // 2
    inv_freq = 1.0 / (ROPE_BASE ** (jnp.arange(half, dtype=jnp.float32) / half))
    ang = positions.astype(jnp.float32)[..., None] * inv_freq
    cos, sin = jnp.cos(ang), jnp.sin(ang)
    q_pe = apply_rope(q_pe, cos[:, :, None, :], sin[:, :, None, :])
    k_pe = apply_rope(k_rope, cos, sin)[:, :, None, :]
    q_full = jnp.concatenate([q_nope, q_pe], axis=-1)
    k_full = jnp.concatenate(
        [k_nope, jnp.broadcast_to(k_pe, (b, s, MLA_HEADS, MLA_ROPE))], axis=-1)
    y_b = block_attention(q_full, k_full, v)

    return jnp.concatenate([y_a, y_b], axis=-1) @ w_out


def conformer_conv(h, w_pw1, b_pw1, w_dw, b_dw, ln_g, ln_b, w_pw2, b_pw2):
    u = h @ w_pw1 + b_pw1
    a, g = jnp.split(u, 2, axis=-1)
    u = a * jax.nn.sigmoid(g)
    u = depthwise_conv(u, w_dw) + b_dw
    u = jax.nn.silu(layer_norm(u, ln_g, ln_b))
    return u @ w_pw2 + b_pw2


def squared_relu_mlp(h, w1, w2):
    return jnp.square(jax.nn.relu(h @ w1)) @ w2


def setup_inputs(seed: int = 0) -> dict:
    key = jax.random.key(seed)
    ks = jax.random.split(key, 24)

    def nrm(k, shape, fan_in):
        return jax.random.normal(k, shape, jnp.float32) * (fan_in ** -0.5)

    def gain(k, shape):
        return 1.0 + 0.05 * jax.random.normal(k, shape, jnp.float32)

    def bias(k, shape):
        return 0.02 * jax.random.normal(k, shape, jnp.float32)

    x = jax.random.normal(ks[0], (BATCH, SEQ, D_MODEL), jnp.float32)
    offsets = jax.random.randint(ks[1], (BATCH, 1), 0, SEQ, dtype=jnp.int32)
    positions = jnp.arange(SEQ, dtype=jnp.int32)[None, :] + offsets
    return {
        "x": x,
        "positions": positions,
        "sandwich_gains": gain(ks[2], (DEPTH, 4, D_MODEL)),
        "even_w_in": nrm(ks[3], (N_EVEN, D_MODEL, IN_COLS), D_MODEL),
        "even_sc_kernel": nrm(ks[4], (N_EVEN, SC_KERNEL, SC_WIDTH), SC_KERNEL),
        "even_q_norm": gain(ks[5], (N_EVEN, MLA_Q_RANK)),
        "even_w_uq": nrm(ks[6], (N_EVEN, MLA_Q_RANK, MLA_HEADS * (MLA_NOPE + MLA_ROPE)), MLA_Q_RANK),
        "even_kv_norm": gain(ks[7], (N_EVEN, MLA_KV_RANK)),
        "even_w_ukv": nrm(ks[8], (N_EVEN, MLA_KV_RANK, MLA_HEADS * (MLA_NOPE + MLA_V)), MLA_KV_RANK),
        "even_w_out": nrm(ks[9], (N_EVEN, MIX_WIDTH, D_MODEL), MIX_WIDTH),
        "odd_w_pw1": nrm(ks[10], (N_ODD, D_MODEL, 2 * CONF_WIDTH), D_MODEL),
        "odd_b_pw1": bias(ks[11], (N_ODD, 2 * CONF_WIDTH)),
        "odd_w_dw": nrm(ks[12], (N_ODD, CONF_KERNEL, CONF_WIDTH), CONF_KERNEL),
        "odd_b_dw": bias(ks[13], (N_ODD, CONF_WIDTH)),
        "odd_ln_g": gain(ks[14], (N_ODD, CONF_WIDTH)),
        "odd_ln_b": bias(ks[15], (N_ODD, CONF_WIDTH)),
        "odd_w_pw2": nrm(ks[16], (N_ODD, CONF_WIDTH, D_MODEL), CONF_WIDTH),
        "odd_b_pw2": bias(ks[17], (N_ODD, D_MODEL)),
        "mlp_w1": nrm(ks[18], (DEPTH, D_MODEL, D_FF), D_MODEL),
        "mlp_w2": nrm(ks[19], (DEPTH, D_FF, D_MODEL), D_FF),
    }


def reference(x, positions, sandwich_gains, even_w_in, even_sc_kernel, even_q_norm,
              even_w_uq, even_kv_norm, even_w_ukv, even_w_out, odd_w_pw1, odd_b_pw1,
              odd_w_dw, odd_b_dw, odd_ln_g, odd_ln_b, odd_w_pw2, odd_b_pw2,
              mlp_w1, mlp_w2):
    for i in range(DEPTH):
        g = sandwich_gains[i]
        j = i // 2
        h = rms_norm(x, g[0])
        if i % 2 == 0:
            m = parallel_conv_mla(h, positions, even_w_in[j], even_sc_kernel[j],
                                  even_q_norm[j], even_w_uq[j], even_kv_norm[j],
                                  even_w_ukv[j], even_w_out[j])
        else:
            m = conformer_conv(h, odd_w_pw1[j], odd_b_pw1[j], odd_w_dw[j], odd_b_dw[j],
                               odd_ln_g[j], odd_ln_b[j], odd_w_pw2[j], odd_b_pw2[j])
        x = x + rms_norm(m, g[1])
        h = rms_norm(x, g[2])
        x = x + rms_norm(squared_relu_mlp(h, mlp_w1[i], mlp_w2[i]), g[3])
    return x
```

```python
import functools
import math

import jax
import jax.numpy as jnp
from jax import lax
from jax.experimental import pallas as pl
from jax.experimental.pallas import tpu as pltpu

F32 = jnp.float32
BF16 = jnp.bfloat16

MLA_HEADS = 8
MLA_NOPE = 64
MLA_ROPE = 32
MLA_V = 64
MLA_Q_RANK = 384
MLA_KV_RANK = 256
SC_WIDTH = 512
ROPE_BASE = 10000.0
EPS = 1e-6

LANES = 128
HEAD_SLOT = LANES
ROPE_LO = MLA_NOPE
HALF = MLA_ROPE // 2
HALO = 16
VMEM_LIMIT = 56 * 1024 * 1024

TS_IN = 512
TQ = 512
TM0 = 512
TM1 = 256
CONV_RB = 64
CONV_CB = 256
FF_CHUNK = 1024

Q_SCALE = float((MLA_NOPE + MLA_ROPE) ** -0.5 * math.log2(math.e))


def _rms(x, g):
    return x * lax.rsqrt(jnp.mean(x * x, axis=-1, keepdims=True) + EPS) * g


def _const_spec(shape):
    nd = len(shape)
    return pl.BlockSpec(shape, lambda *_: (0,) * nd, pipeline_mode=pl.Buffered(1))


def _dot(a, b):
    return jnp.dot(a, b, preferred_element_type=F32)


def _dot_nt(a, b):
    return lax.dot_general(a, b, (((1,), (1,)), ((), ())), preferred_element_type=F32)


def _rope_rows(x1, x2, c, s):
    return x1 * c - x2 * s, x2 * c + x1 * s


def _inproj_kernel(x_ref, pos_ref, invf_ref, g_ref, win_ref, qn_ref, wuqT_ref, kvn_ref,
                   wuk_ref, wuvT_ref, gb_ref, u_ref, qT_ref, k_ref, vT_ref):
    ts = x_ref.shape[1]
    h = _rms(x_ref[0], g_ref[0, 0:1, :]).astype(BF16)
    proj = _dot(h, win_ref[...])
    o = 0
    gb_ref[0] = proj[:, o:o + SC_WIDTH].astype(BF16)
    o += SC_WIDTH
    u_ref[0] = (proj[:, o:o + SC_WIDTH] * proj[:, o + SC_WIDTH:o + 2 * SC_WIDTH]).astype(BF16)
    o += 2 * SC_WIDTH
    qn = _rms(proj[:, o:o + MLA_Q_RANK], qn_ref[...]).astype(BF16)
    o += MLA_Q_RANK
    cn = _rms(proj[:, o:o + MLA_KV_RANK], kvn_ref[...]).astype(BF16)
    o += MLA_KV_RANK
    kr = proj[:, o:o + HEAD_SLOT]

    ang = invf_ref[...] * pos_ref[0].astype(F32)
    c = jnp.cos(ang)
    s = jnp.sin(ang)

    krT = kr.T
    r1, r2 = _rope_rows(krT[ROPE_LO:ROPE_LO + HALF], krT[ROPE_LO + HALF:ROPE_LO + 2 * HALF], c, s)
    kpeT = jnp.concatenate(
        [jnp.zeros((ROPE_LO, ts), F32), r1, r2,
         jnp.zeros((HEAD_SLOT - ROPE_LO - 2 * HALF, ts), F32)], axis=0)
    kpe = kpeT.T

    knope = _dot(cn, wuk_ref[...])
    for hh in range(MLA_HEADS):
        sl = slice(hh * HEAD_SLOT, (hh + 1) * HEAD_SLOT)
        k_ref[0, :, sl] = (knope[:, sl] + kpe).astype(BF16)

    vT_ref[0, 0] = _dot_nt(wuvT_ref[...], cn).astype(BF16)

    qT = _dot_nt(wuqT_ref[...], qn)
    for hh in range(MLA_HEADS):
        b0 = hh * HEAD_SLOT
        r1, r2 = _rope_rows(qT[b0 + ROPE_LO:b0 + ROPE_LO + HALF],
                            qT[b0 + ROPE_LO + HALF:b0 + ROPE_LO + 2 * HALF], c, s)
        blk = jnp.concatenate([qT[b0:b0 + ROPE_LO], r1, r2,
                               qT[b0 + ROPE_LO + 2 * HALF:b0 + HEAD_SLOT]], axis=0)
        qT_ref[0, b0:b0 + HEAD_SLOT, :] = (blk * Q_SCALE).astype(BF16)


def _attn_kernel(qT_ref, k_ref, vT_ref, o_ref, m_ref, acc_ref):
    tq = qT_ref.shape[2]
    n_chunks, tk = vT_ref.shape[1], vT_ref.shape[3]
    m_ref[...] = jnp.full(m_ref.shape, -1e30, F32)
    acc_ref[...] = jnp.zeros(acc_ref.shape, F32)
    ones = jnp.ones((HALO, tk), BF16)

    def body(ci, carry):
        start = pl.multiple_of(ci * tk, tk)
        for a in range(2):
            kc = k_ref[0, pl.ds(start, tk), a * HEAD_SLOT:(a + 1) * HEAD_SLOT]
            s = _dot(kc, qT_ref[0, a * HEAD_SLOT:(a + 1) * HEAD_SLOT, :])
            m_prev = m_ref[a]
            m_new = jnp.maximum(m_prev, jnp.max(s, axis=0, keepdims=True))
            alpha = jnp.exp2(m_prev - m_new)
            p = jnp.exp2(s - m_new).astype(BF16)
            vt = jnp.concatenate([vT_ref[0, ci, a * MLA_V:(a + 1) * MLA_V, :], ones], axis=0)
            acc_ref[a] = alpha * acc_ref[a] + _dot(vt, p)
            m_ref[a] = m_new
        return carry

    lax.fori_loop(0, n_chunks, body, 0)
    outs = []
    for a in range(2):
        acc = acc_ref[a]
        outs.append(acc[:MLA_V] / acc[MLA_V:MLA_V + 1])
    o_ref[0] = jnp.concatenate(outs, axis=0).T.astype(BF16)


def _mlp(h2, w1_ref, w2_ref):
    acc = None
    for c0 in range(0, w1_ref.shape[1], FF_CHUNK):
        hid = _dot(h2, w1_ref[:, c0:c0 + FF_CHUNK])
        a = jnp.square(jnp.maximum(hid, 0.0)).astype(BF16)
        part = _dot(a, w2_ref[c0:c0 + FF_CHUNK, :])
        acc = part if acc is None else acc + part
    return acc


def _sandwich_tail(x, m, g_ref, w1_ref, w2_ref):
    x1 = x + _rms(m, g_ref[0, 1:2, :])
    h2 = _rms(x1, g_ref[0, 2:3, :]).astype(BF16)
    return x1 + _rms(_mlp(h2, w1_ref, w2_ref), g_ref[0, 3:4, :])


def _tail0_kernel(x_ref, gb_ref, u_ref, up_ref, un_ref, yb_ref, sck_ref, g_ref, gn_ref,
                  woa_ref, wob_ref, w1_ref, w2_ref, wpw1_ref, bpw1_ref, x2_ref, u1_ref):
    i = pl.program_id(1)
    tm = x_ref.shape[1]
    u = u_ref[0].astype(F32)
    prev_row = jnp.where(i > 0, up_ref[0].astype(F32)[HALO - 1:HALO, :], 0.0)
    next_row = jnp.where(i < pl.num_programs(1) - 1, un_ref[0].astype(F32)[0:1, :], 0.0)
    row = lax.broadcasted_iota(jnp.int32, u.shape, 0)
    u_m1 = jnp.where(row == 0, prev_row, pltpu.roll(u, 1, 0))
    u_p1 = jnp.where(row == tm - 1, next_row, pltpu.roll(u, tm - 1, 0))
    conv = sck_ref[0:1, :] * u_m1 + sck_ref[1:2, :] * u + sck_ref[2:3, :] * u_p1
    ya = (gb_ref[0].astype(F32) * conv).astype(BF16)
    m = _dot(ya, woa_ref[...]) + _dot(yb_ref[0], wob_ref[...])
    x2 = _sandwich_tail(x_ref[0], m, g_ref, w1_ref, w2_ref)
    x2_ref[0] = x2
    hn = _rms(x2, gn_ref[0, 0:1, :]).astype(BF16)
    pu = _dot(hn, wpw1_ref[...]) + bpw1_ref[...]
    cw = pu.shape[1] // 2
    u1_ref[0] = pu[:, :cw] * jax.nn.sigmoid(pu[:, cw:])


def _tail1_kernel(x_ref, u_ref, up_ref, un_ref, wdw_ref, bdw_ref, lng_ref, lnb_ref, wpw2_ref,
                  bpw2_ref, g_ref, w1_ref, w2_ref, o_ref, ext_ref, conv_ref):
    i = pl.program_id(1)
    tm, d = u_ref.shape[1], u_ref.shape[2]
    taps = wdw_ref.shape[0]
    pad = taps // 2
    ext_ref[0:HALO, :] = jnp.where(i > 0, up_ref[0], 0.0)
    ext_ref[HALO:HALO + tm, :] = u_ref[0]
    ext_ref[HALO + tm:HALO + tm + HALO, :] = jnp.where(i < pl.num_programs(1) - 1, un_ref[0], 0.0)
    for r0 in range(0, tm, CONV_RB):
        for c0 in range(0, d, CONV_CB):
            acc = jnp.zeros((CONV_RB, CONV_CB), F32)
            for k in range(taps):
                acc = acc + wdw_ref[k:k + 1, c0:c0 + CONV_CB] * \
                    ext_ref[r0 + HALO - pad + k:r0 + HALO - pad + k + CONV_RB, c0:c0 + CONV_CB]
            conv_ref[r0:r0 + CONV_RB, c0:c0 + CONV_CB] = acc + bdw_ref[:, c0:c0 + CONV_CB]
    v = conv_ref[...]
    mu = jnp.mean(v, axis=-1, keepdims=True)
    vc = v - mu
    var = jnp.mean(vc * vc, axis=-1, keepdims=True)
    y = vc * lax.rsqrt(var + EPS) * lng_ref[...] + lnb_ref[...]
    y = (y * jax.nn.sigmoid(y)).astype(BF16)
    m = _dot(y, wpw2_ref[...]) + bpw2_ref[...]
    o_ref[0] = _sandwich_tail(x_ref[0], m, g_ref, w1_ref, w2_ref)


def _halo_specs(tm, width, n_rows):
    per = tm // HALO
    last = n_rows // HALO - 1
    prev = pl.BlockSpec((1, HALO, width), lambda b, i: (b, jnp.maximum(i * per - 1, 0), 0))
    nxt = pl.BlockSpec((1, HALO, width), lambda b, i: (b, jnp.minimum((i + 1) * per, last), 0))
    return prev, nxt


def _params(sem):
    return pltpu.CompilerParams(dimension_semantics=sem, vmem_limit_bytes=VMEM_LIMIT)


def kernel(x, positions, sandwich_gains, even_w_in, even_sc_kernel, even_q_norm, even_w_uq, even_kv_norm, even_w_ukv, even_w_out, odd_w_pw1, odd_b_pw1, odd_w_dw, odd_b_dw, odd_ln_g, odd_ln_b, odd_w_pw2, odd_b_pw2, mlp_w1, mlp_w2):
    B, S, D = x.shape
    H = MLA_HEADS
    d_ff = mlp_w1.shape[2]
    n_in = S // TS_IN
    assert S % TS_IN == 0 and S % TQ == 0 and S % TM0 == 0 and S % TM1 == 0

    w_in = even_w_in[0]
    lat_end = 3 * SC_WIDTH + MLA_Q_RANK + MLA_KV_RANK
    win_p = jnp.concatenate(
        [w_in[:, :lat_end], jnp.zeros((D, ROPE_LO), F32), w_in[:, lat_end:],
         jnp.zeros((D, HEAD_SLOT - ROPE_LO - MLA_ROPE), F32)], axis=1).astype(BF16)
    w_uq = even_w_uq[0].reshape(MLA_Q_RANK, H, MLA_NOPE + MLA_ROPE)
    wuqT = jnp.pad(w_uq, ((0, 0), (0, 0), (0, HEAD_SLOT - MLA_NOPE - MLA_ROPE))
                   ).reshape(MLA_Q_RANK, H * HEAD_SLOT).T.astype(BF16)
    w_ukv = even_w_ukv[0].reshape(MLA_KV_RANK, H, MLA_NOPE + MLA_V)
    wuk = jnp.pad(w_ukv[:, :, :MLA_NOPE], ((0, 0), (0, 0), (0, HEAD_SLOT - MLA_NOPE))
                  ).reshape(MLA_KV_RANK, H * HEAD_SLOT).astype(BF16)
    wuvT = w_ukv[:, :, MLA_NOPE:].reshape(MLA_KV_RANK, H * MLA_V).T.astype(BF16)
    woa = even_w_out[0, :SC_WIDTH].astype(BF16)
    wob = even_w_out[0, SC_WIDTH:].astype(BF16)
    w1 = mlp_w1.astype(BF16)
    w2 = mlp_w2.astype(BF16)
    wpw1 = odd_w_pw1[0].astype(BF16)
    wpw2 = odd_w_pw2[0].astype(BF16)
    inv_freq = 1.0 / (ROPE_BASE ** (jnp.arange(HALF, dtype=F32) / HALF))
    invf = jnp.broadcast_to(inv_freq[:, None], (HALF, TS_IN))
    pos3 = positions.reshape(B, 1, S)
    g0 = sandwich_gains[0:1]
    g1 = sandwich_gains[1:2]

    seq_tile = lambda w: pl.BlockSpec((1, TS_IN, w), lambda b, i: (b, i, 0))
    gb, u, qT, k, vT = pl.pallas_call(
        _inproj_kernel,
        grid=(B, n_in),
        in_specs=[
            seq_tile(D),
            pl.BlockSpec((1, 1, TS_IN), lambda b, i: (b, 0, i)),
            _const_spec((HALF, TS_IN)),
            _const_spec((1, 4, D)),
            _const_spec(win_p.shape),
            _const_spec((1, MLA_Q_RANK)),
            _const_spec(wuqT.shape),
            _const_spec((1, MLA_KV_RANK)),
            _const_spec(wuk.shape),
            _const_spec(wuvT.shape),
        ],
        out_specs=[
            seq_tile(SC_WIDTH),
            seq_tile(SC_WIDTH),
            pl.BlockSpec((1, H * HEAD_SLOT, TS_IN), lambda b, i: (b, 0, i)),
            seq_tile(H * HEAD_SLOT),
            pl.BlockSpec((1, 1, H * MLA_V, TS_IN), lambda b, i: (b, i, 0, 0)),
        ],
        out_shape=[
            jax.ShapeDtypeStruct((B, S, SC_WIDTH), BF16),
            jax.ShapeDtypeStruct((B, S, SC_WIDTH), BF16),
            jax.ShapeDtypeStruct((B, H * HEAD_SLOT, S), BF16),
            jax.ShapeDtypeStruct((B, S, H * HEAD_SLOT), BF16),
            jax.ShapeDtypeStruct((B, n_in, H * MLA_V, TS_IN), BF16),
        ],
        compiler_params=_params(("parallel", "parallel")),
        name="inproj",
    )(x, pos3, invf, g0, win_p, even_q_norm, wuqT, even_kv_norm, wuk, wuvT)

    yb = pl.pallas_call(
        _attn_kernel,
        grid=(B, H // 2, S // TQ),
        in_specs=[
            pl.BlockSpec((1, 2 * HEAD_SLOT, TQ), lambda b, j, i: (b, j, i)),
            pl.BlockSpec((1, S, 2 * HEAD_SLOT), lambda b, j, i: (b, 0, j)),
            pl.BlockSpec((1, n_in, 2 * MLA_V, TS_IN), lambda b, j, i: (b, 0, j, 0)),
        ],
        out_specs=pl.BlockSpec((1, TQ, 2 * MLA_V), lambda b, j, i: (b, i, j)),
        out_shape=jax.ShapeDtypeStruct((B, S, H * MLA_V), BF16),
        scratch_shapes=[
            pltpu.VMEM((2, 1, TQ), F32),
            pltpu.VMEM((2, MLA_V + HALO, TQ), F32),
        ],
        compiler_params=_params(("parallel", "parallel", "arbitrary")),
        name="attn",
    )(qT, k, vT)

    row0 = lambda w: pl.BlockSpec((1, TM0, w), lambda b, i: (b, i, 0))
    up0, un0 = _halo_specs(TM0, SC_WIDTH, S)
    x2, u1 = pl.pallas_call(
        _tail0_kernel,
        grid=(B, S // TM0),
        in_specs=[
            row0(D), row0(SC_WIDTH), row0(SC_WIDTH), up0, un0, row0(H * MLA_V),
            _const_spec((3, SC_WIDTH)),
            _const_spec((1, 4, D)),
            _const_spec((1, 4, D)),
            _const_spec(woa.shape), _const_spec(wob.shape),
            pl.BlockSpec((None, D, d_ff), lambda b, i: (0, 0, 0), pipeline_mode=pl.Buffered(1)),
            pl.BlockSpec((None, d_ff, D), lambda b, i: (0, 0, 0), pipeline_mode=pl.Buffered(1)),
            _const_spec(wpw1.shape),
            _const_spec((1, 2 * D)),
        ],
        out_specs=[row0(D), row0(D)],
        out_shape=[jax.ShapeDtypeStruct((B, S, D), F32), jax.ShapeDtypeStruct((B, S, D), F32)],
        compiler_params=_params(("parallel", "parallel")),
        name="tail0",
    )(x, gb, u, u, u, yb, even_sc_kernel[0], g0, g1, woa, wob, w1, w2, wpw1, odd_b_pw1)

    row1 = lambda w: pl.BlockSpec((1, TM1, w), lambda b, i: (b, i, 0))
    up1, un1 = _halo_specs(TM1, D, S)
    taps = odd_w_dw.shape[1]
    out = pl.pallas_call(
        _tail1_kernel,
        grid=(B, S // TM1),
        in_specs=[
            row1(D), row1(D), up1, un1,
            _const_spec((taps, D)),
            _const_spec((1, D)), _const_spec((1, D)), _const_spec((1, D)),
            _const_spec(wpw2.shape),
            _const_spec((1, D)),
            _const_spec((1, 4, D)),
            pl.BlockSpec((None, D, d_ff), lambda b, i: (1, 0, 0), pipeline_mode=pl.Buffered(1)),
            pl.BlockSpec((None, d_ff, D), lambda b, i: (1, 0, 0), pipeline_mode=pl.Buffered(1)),
        ],
        out_specs=row1(D),
        out_shape=jax.ShapeDtypeStruct((B, S, D), F32),
        scratch_shapes=[
            pltpu.VMEM((TM1 + 2 * HALO, D), F32),
            pltpu.VMEM((TM1, D), F32),
        ],
        compiler_params=_params(("parallel", "parallel")),
        name="tail1",
    )(x2, u1, u1, u1, odd_w_dw[0], odd_b_dw, odd_ln_g, odd_ln_b, wpw2, odd_b_pw2, g1, w1, w2)
    return out
```

```python
import functools
import math

import jax
import jax.numpy as jnp
from jax import lax
from jax.experimental import pallas as pl
from jax.experimental.pallas import tpu as pltpu

F32 = jnp.float32
BF16 = jnp.bfloat16

MLA_HEADS = 8
MLA_NOPE = 64
MLA_ROPE = 32
MLA_V = 64
MLA_Q_RANK = 384
MLA_KV_RANK = 256
SC_WIDTH = 512
ROPE_BASE = 10000.0
EPS = 1e-6

LANES = 128
SUBLANES = 8
HEAD_SLOT = LANES
ROPE_LO = MLA_NOPE
HALF = MLA_ROPE // 2
HALO = 16
VMEM_LIMIT = 56 * 1024 * 1024

TS_IN = 512
TQ = 512
TM0 = 512
TM1 = 256
CONV_RB = 64
CONV_CB = 256
FF_CHUNK = 1024

Q_SCALE = float((MLA_NOPE + MLA_ROPE) ** -0.5 * math.log2(math.e))


def _rms(x, g):
    return x * lax.rsqrt(jnp.mean(x * x, axis=-1, keepdims=True) + EPS) * g


def _const_spec(shape):
    nd = len(shape)
    return pl.BlockSpec(shape, lambda *_: (0,) * nd, pipeline_mode=pl.Buffered(1))


def _dot(a, b):
    return jnp.dot(a, b, preferred_element_type=F32)


def _dot_nt(a, b):
    return lax.dot_general(a, b, (((1,), (1,)), ((), ())), preferred_element_type=F32)


def _rope_rows(x1, x2, c, s):
    return x1 * c - x2 * s, x2 * c + x1 * s


def _inproj_kernel(x_ref, pos_ref, invf_ref, g_ref, win_ref, qn_ref, wuqT_ref, kvn_ref,
                   wuk_ref, wuvT_ref, gb_ref, u_ref, qT_ref, k_ref, vT_ref):
    ts = x_ref.shape[1]
    h = _rms(x_ref[0], g_ref[0, 0:1, :]).astype(BF16)
    proj = _dot(h, win_ref[...])
    o = 0
    gb_ref[0] = proj[:, o:o + SC_WIDTH].astype(BF16)
    o += SC_WIDTH
    u_ref[0] = (proj[:, o:o + SC_WIDTH] * proj[:, o + SC_WIDTH:o + 2 * SC_WIDTH]).astype(BF16)
    o += 2 * SC_WIDTH
    qn = _rms(proj[:, o:o + MLA_Q_RANK], qn_ref[...]).astype(BF16)
    o += MLA_Q_RANK
    cn = _rms(proj[:, o:o + MLA_KV_RANK], kvn_ref[...]).astype(BF16)
    o += MLA_KV_RANK
    kr = proj[:, o:o + HEAD_SLOT]

    ang = invf_ref[...] * pos_ref[0].astype(F32)
    c = jnp.cos(ang)
    s = jnp.sin(ang)

    krT = kr.T
    r1, r2 = _rope_rows(krT[ROPE_LO:ROPE_LO + HALF], krT[ROPE_LO + HALF:ROPE_LO + 2 * HALF], c, s)
    kpeT = jnp.concatenate(
        [jnp.zeros((ROPE_LO, ts), F32), r1, r2,
         jnp.zeros((HEAD_SLOT - ROPE_LO - 2 * HALF, ts), F32)], axis=0)
    kpe = kpeT.T

    knope = _dot(cn, wuk_ref[...])
    for hh in range(MLA_HEADS):
        sl = slice(hh * HEAD_SLOT, (hh + 1) * HEAD_SLOT)
        k_ref[0, :, sl] = (knope[:, sl] + kpe).astype(BF16)

    vT_ref[0, 0] = _dot_nt(wuvT_ref[...], cn).astype(BF16)

    qT = _dot_nt(wuqT_ref[...], qn)
    for hh in range(MLA_HEADS):
        b0 = hh * HEAD_SLOT
        r1, r2 = _rope_rows(qT[b0 + ROPE_LO:b0 + ROPE_LO + HALF],
                            qT[b0 + ROPE_LO + HALF:b0 + ROPE_LO + 2 * HALF], c, s)
        blk = jnp.concatenate([qT[b0:b0 + ROPE_LO], r1, r2,
                               qT[b0 + ROPE_LO + 2 * HALF:b0 + HEAD_SLOT]], axis=0)
        qT_ref[0, b0:b0 + HEAD_SLOT, :] = (blk * Q_SCALE).astype(BF16)


def _attn_kernel(qT_ref, k_ref, vT_ref, o_ref, m_ref, acc_ref, s_ref, cm_ref):
    n_chunks, tk = vT_ref.shape[1], vT_ref.shape[3]
    m_ref[...] = jnp.full(m_ref.shape, -1e30, F32)
    acc_ref[...] = jnp.zeros(acc_ref.shape, F32)
    ones = jnp.ones((HALO, tk), BF16)

    def scores(ci, a, slot):
        start = pl.multiple_of(ci * tk, tk)
        kc = k_ref[0, pl.ds(start, tk), a * HEAD_SLOT:(a + 1) * HEAD_SLOT]
        s = _dot(kc, qT_ref[0, a * HEAD_SLOT:(a + 1) * HEAD_SLOT, :])
        s_ref[slot, a] = s
        cm_ref[slot, a] = jnp.max(s, axis=0, keepdims=True)

    def consume(ci, a, slot):
        m_prev = m_ref[a]
        m_new = jnp.maximum(m_prev, cm_ref[slot, a])
        alpha = jnp.exp2(m_prev - m_new)
        p = jnp.exp2(s_ref[slot, a] - m_new).astype(BF16)
        vt = jnp.concatenate([vT_ref[0, ci, a * MLA_V:(a + 1) * MLA_V, :], ones], axis=0)
        acc_ref[a] = alpha * acc_ref[a] + _dot(vt, p)
        m_ref[a] = m_new

    for a in range(2):
        scores(0, a, 0)

    def body(c2, carry):
        for slot in range(2):
            ci = 2 * c2 + slot
            nxt = jnp.minimum(ci + 1, n_chunks - 1)
            for a in range(2):
                scores(nxt, a, 1 - slot)
                consume(ci, a, slot)
        return carry

    lax.fori_loop(0, n_chunks // 2, body, 0)
    outs = []
    for a in range(2):
        acc = acc_ref[a]
        outs.append(acc[:MLA_V] / acc[MLA_V:MLA_V + 1])
    o_ref[0] = jnp.concatenate(outs, axis=0).T.astype(BF16)


def _mlp(h2, w1_ref, w2_ref):
    acc = None
    for c0 in range(0, w1_ref.shape[1], FF_CHUNK):
        hid = _dot(h2, w1_ref[:, c0:c0 + FF_CHUNK])
        a = jnp.square(jnp.maximum(hid, 0.0)).astype(BF16)
        part = _dot(a, w2_ref[c0:c0 + FF_CHUNK, :])
        acc = part if acc is None else acc + part
    return acc


def _sandwich_tail(x, m, g_ref, w1_ref, w2_ref):
    x1 = x + _rms(m, g_ref[0, 1:2, :])
    h2 = _rms(x1, g_ref[0, 2:3, :]).astype(BF16)
    return x1 + _rms(_mlp(h2, w1_ref, w2_ref), g_ref[0, 3:4, :])


def _tail0_kernel(x_ref, gb_ref, u_ref, up_ref, un_ref, yb_ref, sck_ref, g_ref, gn_ref,
                  woa_ref, wob_ref, w1_ref, w2_ref, wpw1_ref, bpw1_ref, x2_ref, u1_ref):
    i = pl.program_id(1)
    tm = x_ref.shape[1]
    u = u_ref[0].astype(F32)
    prev_row = jnp.where(i > 0, up_ref[0].astype(F32)[HALO - 1:HALO, :], 0.0)
    next_row = jnp.where(i < pl.num_programs(1) - 1, un_ref[0].astype(F32)[0:1, :], 0.0)
    row = lax.broadcasted_iota(jnp.int32, u.shape, 0)
    u_m1 = jnp.where(row == 0, prev_row, pltpu.roll(u, 1, 0))
    u_p1 = jnp.where(row == tm - 1, next_row, pltpu.roll(u, tm - 1, 0))
    conv = sck_ref[0:1, :] * u_m1 + sck_ref[1:2, :] * u + sck_ref[2:3, :] * u_p1
    ya = (gb_ref[0].astype(F32) * conv).astype(BF16)
    m = _dot(ya, woa_ref[...]) + _dot(yb_ref[0], wob_ref[...])
    x2 = _sandwich_tail(x_ref[0], m, g_ref, w1_ref, w2_ref)
    x2_ref[0] = x2
    hn = _rms(x2, gn_ref[0, 0:1, :]).astype(BF16)
    pu = _dot(hn, wpw1_ref[...]) + bpw1_ref[...]
    cw = pu.shape[1] // 2
    u1_ref[0] = pu[:, :cw] * jax.nn.sigmoid(pu[:, cw:])


def _tail1_kernel(x_ref, u_ref, up_ref, un_ref, wdw_ref, bdw_ref, lng_ref, lnb_ref, wpw2_ref,
                  bpw2_ref, g_ref, w1_ref, w2_ref, o_ref, ext_ref, sh_ref, conv_ref):
    i = pl.program_id(1)
    tm, d = u_ref.shape[1], u_ref.shape[2]
    taps = wdw_ref.shape[0]
    pad = taps // 2
    ext_ref[0:HALO, :] = jnp.where(i > 0, up_ref[0], 0.0)
    ext_ref[HALO:HALO + tm, :] = u_ref[0]
    ext_ref[HALO + tm:HALO + tm + HALO, :] = jnp.where(i < pl.num_programs(1) - 1, un_ref[0], 0.0)
    n_sh = tm + SUBLANES * ((HALO - pad + taps - 1) // SUBLANES)
    for c0 in range(0, d, CONV_CB):
        cols = slice(c0, c0 + CONV_CB)
        for rho in range(1, SUBLANES):
            sh_ref[rho - 1, 0:n_sh, :] = ext_ref[rho:rho + n_sh, cols]
        for r0 in range(0, tm, CONV_RB):
            acc = jnp.zeros((CONV_RB, CONV_CB), F32)
            for k in range(taps):
                off = HALO - pad + k
                rho, q = off % SUBLANES, off - off % SUBLANES
                if rho == 0:
                    src = ext_ref[r0 + q:r0 + q + CONV_RB, cols]
                else:
                    src = sh_ref[rho - 1, r0 + q:r0 + q + CONV_RB, :]
                acc = acc + wdw_ref[k:k + 1, cols] * src
            conv_ref[r0:r0 + CONV_RB, cols] = acc + bdw_ref[:, cols]
    v = conv_ref[...]
    mu = jnp.mean(v, axis=-1, keepdims=True)
    vc = v - mu
    var = jnp.mean(vc * vc, axis=-1, keepdims=True)
    y = vc * lax.rsqrt(var + EPS) * lng_ref[...] + lnb_ref[...]
    y = (y * jax.nn.sigmoid(y)).astype(BF16)
    m = _dot(y, wpw2_ref[...]) + bpw2_ref[...]
    o_ref[0] = _sandwich_tail(x_ref[0], m, g_ref, w1_ref, w2_ref)


def _halo_specs(tm, width, n_rows):
    per = tm // HALO
    last = n_rows // HALO - 1
    prev = pl.BlockSpec((1, HALO, width), lambda b, i: (b, jnp.maximum(i * per - 1, 0), 0))
    nxt = pl.BlockSpec((1, HALO, width), lambda b, i: (b, jnp.minimum((i + 1) * per, last), 0))
    return prev, nxt


def _params(sem):
    return pltpu.CompilerParams(dimension_semantics=sem, vmem_limit_bytes=VMEM_LIMIT)


def kernel(x, positions, sandwich_gains, even_w_in, even_sc_kernel, even_q_norm, even_w_uq, even_kv_norm, even_w_ukv, even_w_out, odd_w_pw1, odd_b_pw1, odd_w_dw, odd_b_dw, odd_ln_g, odd_ln_b, odd_w_pw2, odd_b_pw2, mlp_w1, mlp_w2):
    B, S, D = x.shape
    H = MLA_HEADS
    d_ff = mlp_w1.shape[2]
    n_in = S // TS_IN
    assert S % TS_IN == 0 and S % TQ == 0 and S % TM0 == 0 and S % TM1 == 0

    w_in = even_w_in[0]
    lat_end = 3 * SC_WIDTH + MLA_Q_RANK + MLA_KV_RANK
    win_p = jnp.concatenate(
        [w_in[:, :lat_end], jnp.zeros((D, ROPE_LO), F32), w_in[:, lat_end:],
         jnp.zeros((D, HEAD_SLOT - ROPE_LO - MLA_ROPE), F32)], axis=1).astype(BF16)
    w_uq = even_w_uq[0].reshape(MLA_Q_RANK, H, MLA_NOPE + MLA_ROPE)
    wuqT = jnp.pad(w_uq, ((0, 0), (0, 0), (0, HEAD_SLOT - MLA_NOPE - MLA_ROPE))
                   ).reshape(MLA_Q_RANK, H * HEAD_SLOT).T.astype(BF16)
    w_ukv = even_w_ukv[0].reshape(MLA_KV_RANK, H, MLA_NOPE + MLA_V)
    wuk = jnp.pad(w_ukv[:, :, :MLA_NOPE], ((0, 0), (0, 0), (0, HEAD_SLOT - MLA_NOPE))
                  ).reshape(MLA_KV_RANK, H * HEAD_SLOT).astype(BF16)
    wuvT = w_ukv[:, :, MLA_NOPE:].reshape(MLA_KV_RANK, H * MLA_V).T.astype(BF16)
    woa = even_w_out[0, :SC_WIDTH].astype(BF16)
    wob = even_w_out[0, SC_WIDTH:].astype(BF16)
    w1 = mlp_w1.astype(BF16)
    w2 = mlp_w2.astype(BF16)
    wpw1 = odd_w_pw1[0].astype(BF16)
    wpw2 = odd_w_pw2[0].astype(BF16)
    inv_freq = 1.0 / (ROPE_BASE ** (jnp.arange(HALF, dtype=F32) / HALF))
    invf = jnp.broadcast_to(inv_freq[:, None], (HALF, TS_IN))
    pos3 = positions.reshape(B, 1, S)
    g0 = sandwich_gains[0:1]
    g1 = sandwich_gains[1:2]

    seq_tile = lambda w: pl.BlockSpec((1, TS_IN, w), lambda b, i: (b, i, 0))
    gb, u, qT, k, vT = pl.pallas_call(
        _inproj_kernel,
        grid=(B, n_in),
        in_specs=[
            seq_tile(D),
            pl.BlockSpec((1, 1, TS_IN), lambda b, i: (b, 0, i)),
            _const_spec((HALF, TS_IN)),
            _const_spec((1, 4, D)),
            _const_spec(win_p.shape),
            _const_spec((1, MLA_Q_RANK)),
            _const_spec(wuqT.shape),
            _const_spec((1, MLA_KV_RANK)),
            _const_spec(wuk.shape),
            _const_spec(wuvT.shape),
        ],
        out_specs=[
            seq_tile(SC_WIDTH),
            seq_tile(SC_WIDTH),
            pl.BlockSpec((1, H * HEAD_SLOT, TS_IN), lambda b, i: (b, 0, i)),
            seq_tile(H * HEAD_SLOT),
            pl.BlockSpec((1, 1, H * MLA_V, TS_IN), lambda b, i: (b, i, 0, 0)),
        ],
        out_shape=[
            jax.ShapeDtypeStruct((B, S, SC_WIDTH), BF16),
            jax.ShapeDtypeStruct((B, S, SC_WIDTH), BF16),
            jax.ShapeDtypeStruct((B, H * HEAD_SLOT, S), BF16),
            jax.ShapeDtypeStruct((B, S, H * HEAD_SLOT), BF16),
            jax.ShapeDtypeStruct((B, n_in, H * MLA_V, TS_IN), BF16),
        ],
        compiler_params=_params(("parallel", "parallel")),
        name="inproj",
    )(x, pos3, invf, g0, win_p, even_q_norm, wuqT, even_kv_norm, wuk, wuvT)

    yb = pl.pallas_call(
        _attn_kernel,
        grid=(B, H // 2, S // TQ),
        in_specs=[
            pl.BlockSpec((1, 2 * HEAD_SLOT, TQ), lambda b, j, i: (b, j, i)),
            pl.BlockSpec((1, S, 2 * HEAD_SLOT), lambda b, j, i: (b, 0, j)),
            pl.BlockSpec((1, n_in, 2 * MLA_V, TS_IN), lambda b, j, i: (b, 0, j, 0)),
        ],
        out_specs=pl.BlockSpec((1, TQ, 2 * MLA_V), lambda b, j, i: (b, i, j)),
        out_shape=jax.ShapeDtypeStruct((B, S, H * MLA_V), BF16),
        scratch_shapes=[
            pltpu.VMEM((2, 1, TQ), F32),
            pltpu.VMEM((2, MLA_V + HALO, TQ), F32),
            pltpu.VMEM((2, 2, TS_IN, TQ), F32),
            pltpu.VMEM((2, 2, 1, TQ), F32),
        ],
        compiler_params=_params(("parallel", "parallel", "arbitrary")),
        name="attn",
    )(qT, k, vT)

    row0 = lambda w: pl.BlockSpec((1, TM0, w), lambda b, i: (b, i, 0))
    up0, un0 = _halo_specs(TM0, SC_WIDTH, S)
    x2, u1 = pl.pallas_call(
        _tail0_kernel,
        grid=(B, S // TM0),
        in_specs=[
            row0(D), row0(SC_WIDTH), row0(SC_WIDTH), up0, un0, row0(H * MLA_V),
            _const_spec((3, SC_WIDTH)),
            _const_spec((1, 4, D)),
            _const_spec((1, 4, D)),
            _const_spec(woa.shape), _const_spec(wob.shape),
            pl.BlockSpec((None, D, d_ff), lambda b, i: (0, 0, 0), pipeline_mode=pl.Buffered(1)),
            pl.BlockSpec((None, d_ff, D), lambda b, i: (0, 0, 0), pipeline_mode=pl.Buffered(1)),
            _const_spec(wpw1.shape),
            _const_spec((1, 2 * D)),
        ],
        out_specs=[row0(D), row0(D)],
        out_shape=[jax.ShapeDtypeStruct((B, S, D), F32), jax.ShapeDtypeStruct((B, S, D), F32)],
        compiler_params=_params(("parallel", "parallel")),
        name="tail0",
    )(x, gb, u, u, u, yb, even_sc_kernel[0], g0, g1, woa, wob, w1, w2, wpw1, odd_b_pw1)

    row1 = lambda w: pl.BlockSpec((1, TM1, w), lambda b, i: (b, i, 0))
    up1, un1 = _halo_specs(TM1, D, S)
    taps = odd_w_dw.shape[1]
    out = pl.pallas_call(
        _tail1_kernel,
        grid=(B, S // TM1),
        in_specs=[
            row1(D), row1(D), up1, un1,
            _const_spec((taps, D)),
            _const_spec((1, D)), _const_spec((1, D)), _const_spec((1, D)),
            _const_spec(wpw2.shape),
            _const_spec((1, D)),
            _const_spec((1, 4, D)),
            pl.BlockSpec((None, D, d_ff), lambda b, i: (1, 0, 0), pipeline_mode=pl.Buffered(1)),
            pl.BlockSpec((None, d_ff, D), lambda b, i: (1, 0, 0), pipeline_mode=pl.Buffered(1)),
        ],
        out_specs=row1(D),
        out_shape=jax.ShapeDtypeStruct((B, S, D), F32),
        scratch_shapes=[
            pltpu.VMEM((TM1 + 2 * HALO, D), F32),
            pltpu.VMEM((SUBLANES - 1, TM1 + 2 * HALO, CONV_CB), F32),
            pltpu.VMEM((TM1, D), F32),
        ],
        compiler_params=_params(("parallel", "parallel")),
        name="tail1",
    )(x2, u1, u1, u1, odd_w_dw[0], odd_b_dw, odd_ln_g, odd_ln_b, wpw2, odd_b_pw2, g1, w1, w2)
    return out
```

```python
import functools
import math

import jax
import jax.numpy as jnp
from jax import lax
from jax.experimental import pallas as pl
from jax.experimental.pallas import tpu as pltpu

F32 = jnp.float32
BF16 = jnp.bfloat16

MLA_HEADS = 8
MLA_NOPE = 64
MLA_ROPE = 32
MLA_V = 64
MLA_Q_RANK = 384
MLA_KV_RANK = 256
SC_WIDTH = 512
ROPE_BASE = 10000.0
EPS = 1e-6

LANES = 128
SUBLANES = 8
HEAD_SLOT = LANES
ROPE_LO = MLA_NOPE
HALF = MLA_ROPE // 2
HALO = 16
VMEM_LIMIT = 56 * 1024 * 1024

TS_IN = 512
TQ = 512
TM0 = 512
TM1 = 512
CONV_RB = 64
CONV_CB = 256
FF_CHUNK = 1024

Q_SCALE = float((MLA_NOPE + MLA_ROPE) ** -0.5 * math.log2(math.e))


def _rms(x, g):
    return x * lax.rsqrt(jnp.mean(x * x, axis=-1, keepdims=True) + EPS) * g


def _const_spec(shape):
    nd = len(shape)
    return pl.BlockSpec(shape, lambda *_: (0,) * nd, pipeline_mode=pl.Buffered(1))


def _dot(a, b):
    return jnp.dot(a, b, preferred_element_type=F32)


def _dot_nt(a, b):
    return lax.dot_general(a, b, (((1,), (1,)), ((), ())), preferred_element_type=F32)


def _rope_rows(x1, x2, c, s):
    return x1 * c - x2 * s, x2 * c + x1 * s


def _inproj_kernel(x_ref, pos_ref, invf_ref, g_ref, win_ref, qn_ref, wuqT_ref, kvn_ref,
                   wuk_ref, wuvT_ref, gb_ref, u_ref, qT_ref, k_ref, vT_ref):
    ts = x_ref.shape[1]
    h = _rms(x_ref[0], g_ref[0, 0:1, :]).astype(BF16)
    proj = _dot(h, win_ref[...])
    o = 0
    gb_ref[0] = proj[:, o:o + SC_WIDTH].astype(BF16)
    o += SC_WIDTH
    u_ref[0] = (proj[:, o:o + SC_WIDTH] * proj[:, o + SC_WIDTH:o + 2 * SC_WIDTH]).astype(BF16)
    o += 2 * SC_WIDTH
    qn = _rms(proj[:, o:o + MLA_Q_RANK], qn_ref[...]).astype(BF16)
    o += MLA_Q_RANK
    cn = _rms(proj[:, o:o + MLA_KV_RANK], kvn_ref[...]).astype(BF16)
    o += MLA_KV_RANK
    kr = proj[:, o:o + HEAD_SLOT]

    ang = invf_ref[...] * pos_ref[0].astype(F32)
    c = jnp.cos(ang)
    s = jnp.sin(ang)

    krT = kr.T
    r1, r2 = _rope_rows(krT[ROPE_LO:ROPE_LO + HALF], krT[ROPE_LO + HALF:ROPE_LO + 2 * HALF], c, s)
    kpeT = jnp.concatenate(
        [jnp.zeros((ROPE_LO, ts), F32), r1, r2,
         jnp.zeros((HEAD_SLOT - ROPE_LO - 2 * HALF, ts), F32)], axis=0)
    kpe = kpeT.T

    knope = _dot(cn, wuk_ref[...])
    for hh in range(MLA_HEADS):
        sl = slice(hh * HEAD_SLOT, (hh + 1) * HEAD_SLOT)
        k_ref[0, :, sl] = (knope[:, sl] + kpe).astype(BF16)

    vT_ref[0, 0] = _dot_nt(wuvT_ref[...], cn).astype(BF16)

    qT = _dot_nt(wuqT_ref[...], qn)
    for hh in range(MLA_HEADS):
        b0 = hh * HEAD_SLOT
        r1, r2 = _rope_rows(qT[b0 + ROPE_LO:b0 + ROPE_LO + HALF],
                            qT[b0 + ROPE_LO + HALF:b0 + ROPE_LO + 2 * HALF], c, s)
        blk = jnp.concatenate([qT[b0:b0 + ROPE_LO], r1, r2,
                               qT[b0 + ROPE_LO + 2 * HALF:b0 + HEAD_SLOT]], axis=0)
        qT_ref[0, b0:b0 + HEAD_SLOT, :] = (blk * Q_SCALE).astype(BF16)


def _attn_kernel(qT_ref, k_ref, vT_ref, o_ref, m_ref, acc_ref, s_ref, cm_ref):
    n_chunks, tk = vT_ref.shape[1], vT_ref.shape[3]
    m_ref[...] = jnp.full(m_ref.shape, -1e30, F32)
    acc_ref[...] = jnp.zeros(acc_ref.shape, F32)
    ones = jnp.ones((HALO, tk), BF16)

    def scores(ci, a, slot):
        start = ci * tk if isinstance(ci, int) else pl.multiple_of(ci * tk, tk)
        kc = k_ref[0, pl.ds(start, tk), a * HEAD_SLOT:(a + 1) * HEAD_SLOT]
        s = _dot(kc, qT_ref[0, a * HEAD_SLOT:(a + 1) * HEAD_SLOT, :])
        s_ref[slot, a] = s
        cm_ref[slot, a] = jnp.max(s, axis=0, keepdims=True)

    def consume(ci, a, slot):
        m_prev = m_ref[a]
        m_new = jnp.maximum(m_prev, cm_ref[slot, a])
        alpha = jnp.exp2(m_prev - m_new)
        p = jnp.exp2(s_ref[slot, a] - m_new).astype(BF16)
        vt = jnp.concatenate([vT_ref[0, ci, a * MLA_V:(a + 1) * MLA_V, :], ones], axis=0)
        acc_ref[a] = alpha * acc_ref[a] + _dot(vt, p)
        m_ref[a] = m_new

    for a in range(2):
        scores(0, a, 0)

    def body(c2, carry):
        for slot in range(2):
            ci = 2 * c2 + slot
            for a in range(2):
                scores(ci + 1, a, 1 - slot)
                consume(ci, a, slot)
        return carry

    lax.fori_loop(0, n_chunks // 2 - 1, body, 0)
    for a in range(2):
        scores(n_chunks - 1, a, 1)
        consume(n_chunks - 2, a, 0)
    for a in range(2):
        consume(n_chunks - 1, a, 1)
    outs = []
    for a in range(2):
        acc = acc_ref[a]
        outs.append(acc[:MLA_V] / acc[MLA_V:MLA_V + 1])
    o_ref[0] = jnp.concatenate(outs, axis=0).T.astype(BF16)


def _mlp(h2, w1_ref, w2_ref):
    acc = None
    for c0 in range(0, w1_ref.shape[1], FF_CHUNK):
        hid = _dot(h2, w1_ref[:, c0:c0 + FF_CHUNK])
        a = jnp.square(jnp.maximum(hid, 0.0)).astype(BF16)
        part = _dot(a, w2_ref[c0:c0 + FF_CHUNK, :])
        acc = part if acc is None else acc + part
    return acc


def _sandwich_tail(x, m, g_ref, w1_ref, w2_ref):
    x1 = x + _rms(m, g_ref[0, 1:2, :])
    h2 = _rms(x1, g_ref[0, 2:3, :]).astype(BF16)
    return x1 + _rms(_mlp(h2, w1_ref, w2_ref), g_ref[0, 3:4, :])


def _tail0_kernel(x_ref, gb_ref, u_ref, up_ref, un_ref, yb_ref, sck_ref, g_ref, gn_ref,
                  woa_ref, wob_ref, w1_ref, w2_ref, wpw1_ref, bpw1_ref, x2_ref, u1_ref):
    i = pl.program_id(1)
    tm = x_ref.shape[1]
    u = u_ref[0].astype(F32)
    prev_row = jnp.where(i > 0, up_ref[0].astype(F32)[HALO - 1:HALO, :], 0.0)
    next_row = jnp.where(i < pl.num_programs(1) - 1, un_ref[0].astype(F32)[0:1, :], 0.0)
    row = lax.broadcasted_iota(jnp.int32, u.shape, 0)
    u_m1 = jnp.where(row == 0, prev_row, pltpu.roll(u, 1, 0))
    u_p1 = jnp.where(row == tm - 1, next_row, pltpu.roll(u, tm - 1, 0))
    conv = sck_ref[0:1, :] * u_m1 + sck_ref[1:2, :] * u + sck_ref[2:3, :] * u_p1
    ya = (gb_ref[0].astype(F32) * conv).astype(BF16)
    m = _dot(ya, woa_ref[...]) + _dot(yb_ref[0], wob_ref[...])
    x2 = _sandwich_tail(x_ref[0], m, g_ref, w1_ref, w2_ref)
    x2_ref[0] = x2
    hn = _rms(x2, gn_ref[0, 0:1, :]).astype(BF16)
    pu = _dot(hn, wpw1_ref[...]) + bpw1_ref[...]
    cw = pu.shape[1] // 2
    u1_ref[0] = pu[:, :cw] * jax.nn.sigmoid(pu[:, cw:])


def _tail1_kernel(x_ref, u_ref, up_ref, un_ref, wdw_ref, bdw_ref, lng_ref, lnb_ref, wpw2_ref,
                  bpw2_ref, g_ref, w1_ref, w2_ref, o_ref, ext_ref, sh_ref, conv_ref):
    i = pl.program_id(1)
    tm, d = u_ref.shape[1], u_ref.shape[2]
    taps = wdw_ref.shape[0] // SUBLANES
    pad = taps // 2
    ext_ref[0:HALO, :] = jnp.where(i > 0, up_ref[0], 0.0)
    ext_ref[HALO:HALO + tm, :] = u_ref[0]
    ext_ref[HALO + tm:HALO + tm + HALO, :] = jnp.where(i < pl.num_programs(1) - 1, un_ref[0], 0.0)
    n_sh = tm + SUBLANES * ((HALO - pad + taps - 1) // SUBLANES)
    for c0 in range(0, d, CONV_CB):
        cols = slice(c0, c0 + CONV_CB)
        for rho in range(1, SUBLANES):
            sh_ref[rho - 1, 0:n_sh, :] = ext_ref[rho:rho + n_sh, cols]
        for r0 in range(0, tm, CONV_RB):
            acc = jnp.zeros((CONV_RB // SUBLANES, SUBLANES, CONV_CB), F32)
            for k in range(taps):
                off = HALO - pad + k
                rho, q = off % SUBLANES, off - off % SUBLANES
                if rho == 0:
                    src = ext_ref[r0 + q:r0 + q + CONV_RB, cols]
                else:
                    src = sh_ref[rho - 1, r0 + q:r0 + q + CONV_RB, :]
                w8 = wdw_ref[k * SUBLANES:(k + 1) * SUBLANES, cols]
                acc = acc + w8[None] * src.reshape(CONV_RB // SUBLANES, SUBLANES, CONV_CB)
            conv_ref[r0:r0 + CONV_RB, cols] = acc.reshape(CONV_RB, CONV_CB) + bdw_ref[:, cols]
    v = conv_ref[...]
    mu = jnp.mean(v, axis=-1, keepdims=True)
    vc = v - mu
    var = jnp.mean(vc * vc, axis=-1, keepdims=True)
    y = vc * lax.rsqrt(var + EPS) * lng_ref[...] + lnb_ref[...]
    y = (y * jax.nn.sigmoid(y)).astype(BF16)
    m = _dot(y, wpw2_ref[...]) + bpw2_ref[...]
    o_ref[0] = _sandwich_tail(x_ref[0], m, g_ref, w1_ref, w2_ref)


def _halo_specs(tm, width, n_rows):
    per = tm // HALO
    last = n_rows // HALO - 1
    prev = pl.BlockSpec((1, HALO, width), lambda b, i: (b, jnp.maximum(i * per - 1, 0), 0))
    nxt = pl.BlockSpec((1, HALO, width), lambda b, i: (b, jnp.minimum((i + 1) * per, last), 0))
    return prev, nxt


def _params(sem, flags=None):
    return pltpu.CompilerParams(dimension_semantics=sem, vmem_limit_bytes=VMEM_LIMIT, flags=flags)


def kernel(x, positions, sandwich_gains, even_w_in, even_sc_kernel, even_q_norm, even_w_uq, even_kv_norm, even_w_ukv, even_w_out, odd_w_pw1, odd_b_pw1, odd_w_dw, odd_b_dw, odd_ln_g, odd_ln_b, odd_w_pw2, odd_b_pw2, mlp_w1, mlp_w2):
    B, S, D = x.shape
    H = MLA_HEADS
    d_ff = mlp_w1.shape[2]
    n_in = S // TS_IN
    assert S % TS_IN == 0 and S % TQ == 0 and S % TM0 == 0 and S % TM1 == 0

    w_in = even_w_in[0]
    lat_end = 3 * SC_WIDTH + MLA_Q_RANK + MLA_KV_RANK
    win_p = jnp.concatenate(
        [w_in[:, :lat_end], jnp.zeros((D, ROPE_LO), F32), w_in[:, lat_end:],
         jnp.zeros((D, HEAD_SLOT - ROPE_LO - MLA_ROPE), F32)], axis=1).astype(BF16)
    w_uq = even_w_uq[0].reshape(MLA_Q_RANK, H, MLA_NOPE + MLA_ROPE)
    wuqT = jnp.pad(w_uq, ((0, 0), (0, 0), (0, HEAD_SLOT - MLA_NOPE - MLA_ROPE))
                   ).reshape(MLA_Q_RANK, H * HEAD_SLOT).T.astype(BF16)
    w_ukv = even_w_ukv[0].reshape(MLA_KV_RANK, H, MLA_NOPE + MLA_V)
    wuk = jnp.pad(w_ukv[:, :, :MLA_NOPE], ((0, 0), (0, 0), (0, HEAD_SLOT - MLA_NOPE))
                  ).reshape(MLA_KV_RANK, H * HEAD_SLOT).astype(BF16)
    wuvT = w_ukv[:, :, MLA_NOPE:].reshape(MLA_KV_RANK, H * MLA_V).T.astype(BF16)
    woa = even_w_out[0, :SC_WIDTH].astype(BF16)
    wob = even_w_out[0, SC_WIDTH:].astype(BF16)
    w1 = mlp_w1.astype(BF16)
    w2 = mlp_w2.astype(BF16)
    wpw1 = odd_w_pw1[0].astype(BF16)
    wpw2 = odd_w_pw2[0].astype(BF16)
    inv_freq = 1.0 / (ROPE_BASE ** (jnp.arange(HALF, dtype=F32) / HALF))
    invf = jnp.broadcast_to(inv_freq[:, None], (HALF, TS_IN))
    pos3 = positions.reshape(B, 1, S)
    g0 = sandwich_gains[0:1]
    g1 = sandwich_gains[1:2]

    seq_tile = lambda w: pl.BlockSpec((1, TS_IN, w), lambda b, i: (b, i, 0))
    gb, u, qT, k, vT = pl.pallas_call(
        _inproj_kernel,
        grid=(B, n_in),
        in_specs=[
            seq_tile(D),
            pl.BlockSpec((1, 1, TS_IN), lambda b, i: (b, 0, i)),
            _const_spec((HALF, TS_IN)),
            _const_spec((1, 4, D)),
            _const_spec(win_p.shape),
            _const_spec((1, MLA_Q_RANK)),
            _const_spec(wuqT.shape),
            _const_spec((1, MLA_KV_RANK)),
            _const_spec(wuk.shape),
            _const_spec(wuvT.shape),
        ],
        out_specs=[
            seq_tile(SC_WIDTH),
            seq_tile(SC_WIDTH),
            pl.BlockSpec((1, H * HEAD_SLOT, TS_IN), lambda b, i: (b, 0, i)),
            seq_tile(H * HEAD_SLOT),
            pl.BlockSpec((1, 1, H * MLA_V, TS_IN), lambda b, i: (b, i, 0, 0)),
        ],
        out_shape=[
            jax.ShapeDtypeStruct((B, S, SC_WIDTH), BF16),
            jax.ShapeDtypeStruct((B, S, SC_WIDTH), BF16),
            jax.ShapeDtypeStruct((B, H * HEAD_SLOT, S), BF16),
            jax.ShapeDtypeStruct((B, S, H * HEAD_SLOT), BF16),
            jax.ShapeDtypeStruct((B, n_in, H * MLA_V, TS_IN), BF16),
        ],
        compiler_params=_params(("parallel", "parallel")),
        name="inproj",
    )(x, pos3, invf, g0, win_p, even_q_norm, wuqT, even_kv_norm, wuk, wuvT)

    yb = pl.pallas_call(
        _attn_kernel,
        grid=(B, H // 2, S // TQ),
        in_specs=[
            pl.BlockSpec((1, 2 * HEAD_SLOT, TQ), lambda b, j, i: (b, j, i)),
            pl.BlockSpec((1, S, 2 * HEAD_SLOT), lambda b, j, i: (b, 0, j)),
            pl.BlockSpec((1, n_in, 2 * MLA_V, TS_IN), lambda b, j, i: (b, 0, j, 0)),
        ],
        out_specs=pl.BlockSpec((1, TQ, 2 * MLA_V), lambda b, j, i: (b, i, j)),
        out_shape=jax.ShapeDtypeStruct((B, S, H * MLA_V), BF16),
        scratch_shapes=[
            pltpu.VMEM((2, 1, TQ), F32),
            pltpu.VMEM((2, MLA_V + HALO, TQ), F32),
            pltpu.VMEM((2, 2, TS_IN, TQ), F32),
            pltpu.VMEM((2, 2, 1, TQ), F32),
        ],
        compiler_params=_params(("parallel", "parallel", "arbitrary")),
        name="attn",
    )(qT, k, vT)

    row0 = lambda w: pl.BlockSpec((1, TM0, w), lambda b, i: (b, i, 0))
    up0, un0 = _halo_specs(TM0, SC_WIDTH, S)
    x2, u1 = pl.pallas_call(
        _tail0_kernel,
        grid=(B, S // TM0),
        in_specs=[
            row0(D), row0(SC_WIDTH), row0(SC_WIDTH), up0, un0, row0(H * MLA_V),
            _const_spec((3, SC_WIDTH)),
            _const_spec((1, 4, D)),
            _const_spec((1, 4, D)),
            _const_spec(woa.shape), _const_spec(wob.shape),
            pl.BlockSpec((None, D, d_ff), lambda b, i: (0, 0, 0), pipeline_mode=pl.Buffered(1)),
            pl.BlockSpec((None, d_ff, D), lambda b, i: (0, 0, 0), pipeline_mode=pl.Buffered(1)),
            _const_spec(wpw1.shape),
            _const_spec((1, 2 * D)),
        ],
        out_specs=[row0(D), row0(D)],
        out_shape=[jax.ShapeDtypeStruct((B, S, D), F32), jax.ShapeDtypeStruct((B, S, D), F32)],
        compiler_params=_params(("parallel", "parallel")),
        name="tail0",
    )(x, gb, u, u, u, yb, even_sc_kernel[0], g0, g1, woa, wob, w1, w2, wpw1, odd_b_pw1)

    row1 = lambda w: pl.BlockSpec((1, TM1, w), lambda b, i: (b, i, 0))
    up1, un1 = _halo_specs(TM1, D, S)
    wdw8 = jnp.repeat(odd_w_dw[0], SUBLANES, axis=0)
    out = pl.pallas_call(
        _tail1_kernel,
        grid=(B, S // TM1),
        in_specs=[
            row1(D), row1(D), up1, un1,
            _const_spec(wdw8.shape),
            _const_spec((1, D)), _const_spec((1, D)), _const_spec((1, D)),
            _const_spec(wpw2.shape),
            _const_spec((1, D)),
            _const_spec((1, 4, D)),
            pl.BlockSpec((None, D, d_ff), lambda b, i: (1, 0, 0), pipeline_mode=pl.Buffered(1)),
            pl.BlockSpec((None, d_ff, D), lambda b, i: (1, 0, 0), pipeline_mode=pl.Buffered(1)),
        ],
        out_specs=row1(D),
        out_shape=jax.ShapeDtypeStruct((B, S, D), F32),
        scratch_shapes=[
            pltpu.VMEM((TM1 + 2 * HALO, D), F32),
            pltpu.VMEM((SUBLANES - 1, TM1 + 2 * HALO, CONV_CB), F32),
            pltpu.VMEM((TM1, D), F32),
        ],
        compiler_params=_params(("parallel", "parallel")),
        name="tail1",
    )(x2, u1, u1, u1, wdw8, odd_b_dw, odd_ln_g, odd_ln_b, wpw2, odd_b_pw2, g1, w1, w2)
    return out
```

```python
import functools
import math

import jax
import jax.numpy as jnp
from jax import lax
from jax.experimental import pallas as pl
from jax.experimental.pallas import tpu as pltpu

F32 = jnp.float32
BF16 = jnp.bfloat16

MLA_HEADS = 8
MLA_NOPE = 64
MLA_ROPE = 32
MLA_V = 64
MLA_Q_RANK = 384
MLA_KV_RANK = 256
SC_WIDTH = 512
ROPE_BASE = 10000.0
EPS = 1e-6

LANES = 128
SUBLANES = 8
HEAD_SLOT = LANES
ROPE_LO = MLA_NOPE
HALF = MLA_ROPE // 2
HALO = 16
VMEM_LIMIT = 56 * 1024 * 1024

TS_IN = 512
TQ = 512
KV_UNROLL = 6
TM0 = 512
TM1 = 512
CONV_RB = 64
CONV_CB = 256
FF_CHUNK = 1024

Q_SCALE = float((MLA_NOPE + MLA_ROPE) ** -0.5 * math.log2(math.e))


def _rms(x, g):
    return x * lax.rsqrt(jnp.mean(x * x, axis=-1, keepdims=True) + EPS) * g


def _const_spec(shape):
    nd = len(shape)
    return pl.BlockSpec(shape, lambda *_: (0,) * nd, pipeline_mode=pl.Buffered(1))


def _dot(a, b):
    return jnp.dot(a, b, preferred_element_type=F32)


def _dot_nt(a, b):
    return lax.dot_general(a, b, (((1,), (1,)), ((), ())), preferred_element_type=F32)


def _rope_rows(x1, x2, c, s):
    return x1 * c - x2 * s, x2 * c + x1 * s


def _inproj_kernel(x_ref, pos_ref, invf_ref, g_ref, win_ref, qn_ref, wuqT_ref, kvn_ref,
                   wuk_ref, wuvT_ref, gb_ref, u_ref, qT_ref, k_ref, vT_ref):
    ts = x_ref.shape[1]
    h = _rms(x_ref[0], g_ref[0, 0:1, :]).astype(BF16)
    proj = _dot(h, win_ref[...])
    o = 0
    gb_ref[0] = proj[:, o:o + SC_WIDTH].astype(BF16)
    o += SC_WIDTH
    u_ref[0] = (proj[:, o:o + SC_WIDTH] * proj[:, o + SC_WIDTH:o + 2 * SC_WIDTH]).astype(BF16)
    o += 2 * SC_WIDTH
    qn = _rms(proj[:, o:o + MLA_Q_RANK], qn_ref[...]).astype(BF16)
    o += MLA_Q_RANK
    cn = _rms(proj[:, o:o + MLA_KV_RANK], kvn_ref[...]).astype(BF16)
    o += MLA_KV_RANK
    kr = proj[:, o:o + HEAD_SLOT]

    ang = invf_ref[...] * pos_ref[0].astype(F32)
    c = jnp.cos(ang)
    s = jnp.sin(ang)

    krT = kr.T
    r1, r2 = _rope_rows(krT[ROPE_LO:ROPE_LO + HALF], krT[ROPE_LO + HALF:ROPE_LO + 2 * HALF], c, s)
    kpeT = jnp.concatenate(
        [jnp.zeros((ROPE_LO, ts), F32), r1, r2,
         jnp.zeros((HEAD_SLOT - ROPE_LO - 2 * HALF, ts), F32)], axis=0)
    kpe = kpeT.T

    knope = _dot(cn, wuk_ref[...])
    for hh in range(MLA_HEADS):
        sl = slice(hh * HEAD_SLOT, (hh + 1) * HEAD_SLOT)
        k_ref[0, :, sl] = (knope[:, sl] + kpe).astype(BF16)

    vT_ref[0, 0] = _dot_nt(wuvT_ref[...], cn).astype(BF16)

    qT = _dot_nt(wuqT_ref[...], qn)
    for hh in range(MLA_HEADS):
        b0 = hh * HEAD_SLOT
        r1, r2 = _rope_rows(qT[b0 + ROPE_LO:b0 + ROPE_LO + HALF],
                            qT[b0 + ROPE_LO + HALF:b0 + ROPE_LO + 2 * HALF], c, s)
        blk = jnp.concatenate([qT[b0:b0 + ROPE_LO], r1, r2,
                               qT[b0 + ROPE_LO + 2 * HALF:b0 + HEAD_SLOT]], axis=0)
        qT_ref[0, b0:b0 + HEAD_SLOT, :] = (blk * Q_SCALE).astype(BF16)


def _attn_kernel(qT_ref, qTn_ref, k_ref, vT_ref, o_ref, m_ref, acc_ref, s_ref, cm_ref):
    n_chunks, tk = vT_ref.shape[1], vT_ref.shape[3]
    m_ref[...] = jnp.full(m_ref.shape, -1e30, F32)
    acc_ref[...] = jnp.zeros(acc_ref.shape, F32)
    ones = jnp.ones((HALO, tk), BF16)

    def scores(ci, a, slot, q_ref=qT_ref):
        start = ci * tk if isinstance(ci, int) else pl.multiple_of(ci * tk, tk)
        kc = k_ref[0, pl.ds(start, tk), a * HEAD_SLOT:(a + 1) * HEAD_SLOT]
        s = _dot(kc, q_ref[0, a * HEAD_SLOT:(a + 1) * HEAD_SLOT, :])
        s_ref[slot, a] = s
        cm_ref[slot, a] = jnp.max(s, axis=0, keepdims=True)

    def consume(ci, a, slot):
        m_prev = m_ref[a]
        m_new = jnp.maximum(m_prev, cm_ref[slot, a])
        alpha = jnp.exp2(m_prev - m_new)
        p = jnp.exp2(s_ref[slot, a] - m_new).astype(BF16)
        vt = jnp.concatenate([vT_ref[0, ci, a * MLA_V:(a + 1) * MLA_V, :], ones], axis=0)
        acc_ref[a] = alpha * acc_ref[a] + _dot(vt, p)
        m_ref[a] = m_new

    @pl.when(pl.program_id(2) == 0)
    def _():
        for a in range(2):
            scores(0, a, 0)

    def group(base, size, last):
        for j in range(size):
            ci = base + j
            for a in range(2):
                if last and j == size - 1:
                    scores(0, a, (j + 1) % 2, qTn_ref)
                else:
                    scores(ci + 1, a, (j + 1) % 2)
                consume(ci, a, j % 2)

    def body(g, carry):
        group(g * KV_UNROLL, KV_UNROLL, False)
        return carry

    n_trips = (n_chunks - 2) // KV_UNROLL
    n_tail = n_chunks - n_trips * KV_UNROLL
    assert KV_UNROLL % 2 == 0 and n_tail % 2 == 0
    lax.fori_loop(0, n_trips, body, 0)
    group(n_trips * KV_UNROLL, n_tail, True)
    outs = []
    for a in range(2):
        acc = acc_ref[a]
        outs.append(acc[:MLA_V] / acc[MLA_V:MLA_V + 1])
    o_ref[0] = jnp.concatenate(outs, axis=0).T.astype(BF16)


def _mlp(h2, w1_ref, w2_ref):
    acc = None
    for c0 in range(0, w1_ref.shape[1], FF_CHUNK):
        hid = _dot(h2, w1_ref[:, c0:c0 + FF_CHUNK])
        a = jnp.square(jnp.maximum(hid, 0.0)).astype(BF16)
        part = _dot(a, w2_ref[c0:c0 + FF_CHUNK, :])
        acc = part if acc is None else acc + part
    return acc


def _sandwich_tail(x, m, g_ref, w1_ref, w2_ref):
    x1 = x + _rms(m, g_ref[0, 1:2, :])
    h2 = _rms(x1, g_ref[0, 2:3, :]).astype(BF16)
    return x1 + _rms(_mlp(h2, w1_ref, w2_ref), g_ref[0, 3:4, :])


def _tail0_kernel(x_ref, gb_ref, u_ref, up_ref, un_ref, yb_ref, sck_ref, g_ref, gn_ref,
                  woa_ref, wob_ref, w1_ref, w2_ref, wpw1_ref, bpw1_ref, x2_ref, u1_ref):
    i = pl.program_id(1)
    tm = x_ref.shape[1]
    u = u_ref[0].astype(F32)
    prev_row = jnp.where(i > 0, up_ref[0].astype(F32)[HALO - 1:HALO, :], 0.0)
    next_row = jnp.where(i < pl.num_programs(1) - 1, un_ref[0].astype(F32)[0:1, :], 0.0)
    row = lax.broadcasted_iota(jnp.int32, u.shape, 0)
    u_m1 = jnp.where(row == 0, prev_row, pltpu.roll(u, 1, 0))
    u_p1 = jnp.where(row == tm - 1, next_row, pltpu.roll(u, tm - 1, 0))
    conv = sck_ref[0:1, :] * u_m1 + sck_ref[1:2, :] * u + sck_ref[2:3, :] * u_p1
    ya = (gb_ref[0].astype(F32) * conv).astype(BF16)
    m = _dot(ya, woa_ref[...]) + _dot(yb_ref[0], wob_ref[...])
    x2 = _sandwich_tail(x_ref[0], m, g_ref, w1_ref, w2_ref)
    x2_ref[0] = x2
    hn = _rms(x2, gn_ref[0, 0:1, :]).astype(BF16)
    pu = _dot(hn, wpw1_ref[...]) + bpw1_ref[...]
    cw = pu.shape[1] // 2
    u1_ref[0] = pu[:, :cw] * jax.nn.sigmoid(pu[:, cw:])


def _tail1_kernel(x_ref, u_ref, up_ref, un_ref, wdw_ref, bdw_ref, lng_ref, lnb_ref, wpw2_ref,
                  bpw2_ref, g_ref, w1_ref, w2_ref, o_ref, ext_ref, sh_ref, conv_ref):
    i = pl.program_id(1)
    tm, d = u_ref.shape[1], u_ref.shape[2]
    taps = wdw_ref.shape[0] // SUBLANES
    pad = taps // 2
    ext_ref[0:HALO, :] = jnp.where(i > 0, up_ref[0], 0.0)
    ext_ref[HALO:HALO + tm, :] = u_ref[0]
    ext_ref[HALO + tm:HALO + tm + HALO, :] = jnp.where(i < pl.num_programs(1) - 1, un_ref[0], 0.0)
    n_sh = tm + SUBLANES * ((HALO - pad + taps - 1) // SUBLANES)
    for c0 in range(0, d, CONV_CB):
        cols = slice(c0, c0 + CONV_CB)
        for rho in range(1, SUBLANES):
            sh_ref[rho - 1, 0:n_sh, :] = ext_ref[rho:rho + n_sh, cols]
        for r0 in range(0, tm, CONV_RB):
            acc = jnp.zeros((CONV_RB // SUBLANES, SUBLANES, CONV_CB), F32)
            for k in range(taps):
                off = HALO - pad + k
                rho, q = off % SUBLANES, off - off % SUBLANES
                if rho == 0:
                    src = ext_ref[r0 + q:r0 + q + CONV_RB, cols]
                else:
                    src = sh_ref[rho - 1, r0 + q:r0 + q + CONV_RB, :]
                w8 = wdw_ref[k * SUBLANES:(k + 1) * SUBLANES, cols]
                acc = acc + w8[None] * src.reshape(CONV_RB // SUBLANES, SUBLANES, CONV_CB)
            conv_ref[r0:r0 + CONV_RB, cols] = acc.reshape(CONV_RB, CONV_CB) + bdw_ref[:, cols]
    v = conv_ref[...]
    mu = jnp.mean(v, axis=-1, keepdims=True)
    vc = v - mu
    var = jnp.mean(vc * vc, axis=-1, keepdims=True)
    y = vc * lax.rsqrt(var + EPS) * lng_ref[...] + lnb_ref[...]
    y = (y * jax.nn.sigmoid(y)).astype(BF16)
    m = _dot(y, wpw2_ref[...]) + bpw2_ref[...]
    o_ref[0] = _sandwich_tail(x_ref[0], m, g_ref, w1_ref, w2_ref)


def _halo_specs(tm, width, n_rows):
    per = tm // HALO
    last = n_rows // HALO - 1
    prev = pl.BlockSpec((1, HALO, width), lambda b, i: (b, jnp.maximum(i * per - 1, 0), 0))
    nxt = pl.BlockSpec((1, HALO, width), lambda b, i: (b, jnp.minimum((i + 1) * per, last), 0))
    return prev, nxt


def _params(sem, flags=None):
    return pltpu.CompilerParams(dimension_semantics=sem, vmem_limit_bytes=VMEM_LIMIT, flags=flags)


def kernel(x, positions, sandwich_gains, even_w_in, even_sc_kernel, even_q_norm, even_w_uq, even_kv_norm, even_w_ukv, even_w_out, odd_w_pw1, odd_b_pw1, odd_w_dw, odd_b_dw, odd_ln_g, odd_ln_b, odd_w_pw2, odd_b_pw2, mlp_w1, mlp_w2):
    B, S, D = x.shape
    H = MLA_HEADS
    d_ff = mlp_w1.shape[2]
    n_in = S // TS_IN
    assert S % TS_IN == 0 and S % TQ == 0 and S % TM0 == 0 and S % TM1 == 0

    w_in = even_w_in[0]
    lat_end = 3 * SC_WIDTH + MLA_Q_RANK + MLA_KV_RANK
    win_p = jnp.concatenate(
        [w_in[:, :lat_end], jnp.zeros((D, ROPE_LO), F32), w_in[:, lat_end:],
         jnp.zeros((D, HEAD_SLOT - ROPE_LO - MLA_ROPE), F32)], axis=1).astype(BF16)
    w_uq = even_w_uq[0].reshape(MLA_Q_RANK, H, MLA_NOPE + MLA_ROPE)
    wuqT = jnp.pad(w_uq, ((0, 0), (0, 0), (0, HEAD_SLOT - MLA_NOPE - MLA_ROPE))
                   ).reshape(MLA_Q_RANK, H * HEAD_SLOT).T.astype(BF16)
    w_ukv = even_w_ukv[0].reshape(MLA_KV_RANK, H, MLA_NOPE + MLA_V)
    wuk = jnp.pad(w_ukv[:, :, :MLA_NOPE], ((0, 0), (0, 0), (0, HEAD_SLOT - MLA_NOPE))
                  ).reshape(MLA_KV_RANK, H * HEAD_SLOT).astype(BF16)
    wuvT = w_ukv[:, :, MLA_NOPE:].reshape(MLA_KV_RANK, H * MLA_V).T.astype(BF16)
    woa = even_w_out[0, :SC_WIDTH].astype(BF16)
    wob = even_w_out[0, SC_WIDTH:].astype(BF16)
    w1 = mlp_w1.astype(BF16)
    w2 = mlp_w2.astype(BF16)
    wpw1 = odd_w_pw1[0].astype(BF16)
    wpw2 = odd_w_pw2[0].astype(BF16)
    inv_freq = 1.0 / (ROPE_BASE ** (jnp.arange(HALF, dtype=F32) / HALF))
    invf = jnp.broadcast_to(inv_freq[:, None], (HALF, TS_IN))
    pos3 = positions.reshape(B, 1, S)
    g0 = sandwich_gains[0:1]
    g1 = sandwich_gains[1:2]

    seq_tile = lambda w: pl.BlockSpec((1, TS_IN, w), lambda b, i: (b, i, 0))
    gb, u, qT, k, vT = pl.pallas_call(
        _inproj_kernel,
        grid=(B, n_in),
        in_specs=[
            seq_tile(D),
            pl.BlockSpec((1, 1, TS_IN), lambda b, i: (b, 0, i)),
            _const_spec((HALF, TS_IN)),
            _const_spec((1, 4, D)),
            _const_spec(win_p.shape),
            _const_spec((1, MLA_Q_RANK)),
            _const_spec(wuqT.shape),
            _const_spec((1, MLA_KV_RANK)),
            _const_spec(wuk.shape),
            _const_spec(wuvT.shape),
        ],
        out_specs=[
            seq_tile(SC_WIDTH),
            seq_tile(SC_WIDTH),
            pl.BlockSpec((1, H * HEAD_SLOT, TS_IN), lambda b, i: (b, 0, i)),
            seq_tile(H * HEAD_SLOT),
            pl.BlockSpec((1, 1, H * MLA_V, TS_IN), lambda b, i: (b, i, 0, 0)),
        ],
        out_shape=[
            jax.ShapeDtypeStruct((B, S, SC_WIDTH), BF16),
            jax.ShapeDtypeStruct((B, S, SC_WIDTH), BF16),
            jax.ShapeDtypeStruct((B, H * HEAD_SLOT, S), BF16),
            jax.ShapeDtypeStruct((B, S, H * HEAD_SLOT), BF16),
            jax.ShapeDtypeStruct((B, n_in, H * MLA_V, TS_IN), BF16),
        ],
        compiler_params=_params(("parallel", "parallel")),
        name="inproj",
    )(x, pos3, invf, g0, win_p, even_q_norm, wuqT, even_kv_norm, wuk, wuvT)

    yb = pl.pallas_call(
        _attn_kernel,
        grid=(B, H // 2, S // TQ),
        in_specs=[
            pl.BlockSpec((1, 2 * HEAD_SLOT, TQ), lambda b, j, i: (b, j, i)),
            pl.BlockSpec((1, 2 * HEAD_SLOT, TQ), lambda b, j, i: (b, j, jnp.minimum(i + 1, S // TQ - 1))),
            pl.BlockSpec((1, S, 2 * HEAD_SLOT), lambda b, j, i: (b, 0, j)),
            pl.BlockSpec((1, n_in, 2 * MLA_V, TS_IN), lambda b, j, i: (b, 0, j, 0)),
        ],
        out_specs=pl.BlockSpec((1, TQ, 2 * MLA_V), lambda b, j, i: (b, i, j)),
        out_shape=jax.ShapeDtypeStruct((B, S, H * MLA_V), BF16),
        scratch_shapes=[
            pltpu.VMEM((2, 1, TQ), F32),
            pltpu.VMEM((2, MLA_V + HALO, TQ), F32),
            pltpu.VMEM((2, 2, TS_IN, TQ), F32),
            pltpu.VMEM((2, 2, 1, TQ), F32),
        ],
        compiler_params=_params(("arbitrary", "arbitrary", "arbitrary")),
        name="attn",
    )(qT, qT, k, vT)

    row0 = lambda w: pl.BlockSpec((1, TM0, w), lambda b, i: (b, i, 0))
    up0, un0 = _halo_specs(TM0, SC_WIDTH, S)
    x2, u1 = pl.pallas_call(
        _tail0_kernel,
        grid=(B, S // TM0),
        in_specs=[
            row0(D), row0(SC_WIDTH), row0(SC_WIDTH), up0, un0, row0(H * MLA_V),
            _const_spec((3, SC_WIDTH)),
            _const_spec((1, 4, D)),
            _const_spec((1, 4, D)),
            _const_spec(woa.shape), _const_spec(wob.shape),
            pl.BlockSpec((None, D, d_ff), lambda b, i: (0, 0, 0), pipeline_mode=pl.Buffered(1)),
            pl.BlockSpec((None, d_ff, D), lambda b, i: (0, 0, 0), pipeline_mode=pl.Buffered(1)),
            _const_spec(wpw1.shape),
            _const_spec((1, 2 * D)),
        ],
        out_specs=[row0(D), row0(D)],
        out_shape=[jax.ShapeDtypeStruct((B, S, D), F32), jax.ShapeDtypeStruct((B, S, D), F32)],
        compiler_params=_params(("parallel", "parallel")),
        name="tail0",
    )(x, gb, u, u, u, yb, even_sc_kernel[0], g0, g1, woa, wob, w1, w2, wpw1, odd_b_pw1)

    row1 = lambda w: pl.BlockSpec((1, TM1, w), lambda b, i: (b, i, 0))
    up1, un1 = _halo_specs(TM1, D, S)
    wdw8 = jnp.repeat(odd_w_dw[0], SUBLANES, axis=0)
    out = pl.pallas_call(
        _tail1_kernel,
        grid=(B, S // TM1),
        in_specs=[
            row1(D), row1(D), up1, un1,
            _const_spec(wdw8.shape),
            _const_spec((1, D)), _const_spec((1, D)), _const_spec((1, D)),
            _const_spec(wpw2.shape),
            _const_spec((1, D)),
            _const_spec((1, 4, D)),
            pl.BlockSpec((None, D, d_ff), lambda b, i: (1, 0, 0), pipeline_mode=pl.Buffered(1)),
            pl.BlockSpec((None, d_ff, D), lambda b, i: (1, 0, 0), pipeline_mode=pl.Buffered(1)),
        ],
        out_specs=row1(D),
        out_shape=jax.ShapeDtypeStruct((B, S, D), F32),
        scratch_shapes=[
            pltpu.VMEM((TM1 + 2 * HALO, D), F32),
            pltpu.VMEM((SUBLANES - 1, TM1 + 2 * HALO, CONV_CB), F32),
            pltpu.VMEM((TM1, D), F32),
        ],
        compiler_params=_params(("parallel", "parallel")),
        name="tail1",
    )(x2, u1, u1, u1, wdw8, odd_b_dw, odd_ln_g, odd_ln_b, wpw2, odd_b_pw2, g1, w1, w2)
    return out
```

```python
import functools
import math

import jax
import jax.numpy as jnp
from jax import lax
from jax.experimental import pallas as pl
from jax.experimental.pallas import tpu as pltpu

F32 = jnp.float32
BF16 = jnp.bfloat16

MLA_HEADS = 8
MLA_NOPE = 64
MLA_ROPE = 32
MLA_V = 64
MLA_Q_RANK = 384
MLA_KV_RANK = 256
SC_WIDTH = 512
ROPE_BASE = 10000.0
EPS = 1e-6

LANES = 128
SUBLANES = 8
HEAD_SLOT = LANES
ROPE_LO = MLA_NOPE
HALF = MLA_ROPE // 2
HALO = 16
VMEM_LIMIT = 56 * 1024 * 1024

TS_IN = 512
TQ = 512
KV_UNROLL = 6
ATTN_HEADS = 4
TM0 = 512
TM1 = 512
CONV_RB = 64
CONV_CB = 256
FF_CHUNK = 1024

Q_SCALE = float((MLA_NOPE + MLA_ROPE) ** -0.5 * math.log2(math.e))


def _rms(x, g):
    return x * lax.rsqrt(jnp.mean(x * x, axis=-1, keepdims=True) + EPS) * g


def _const_spec(shape):
    nd = len(shape)
    return pl.BlockSpec(shape, lambda *_: (0,) * nd, pipeline_mode=pl.Buffered(1))


def _dot(a, b):
    return jnp.dot(a, b, preferred_element_type=F32)


def _dot_nt(a, b):
    return lax.dot_general(a, b, (((1,), (1,)), ((), ())), preferred_element_type=F32)


def _rope_rows(x1, x2, c, s):
    return x1 * c - x2 * s, x2 * c + x1 * s


def _inproj_kernel(x_ref, pos_ref, invf_ref, g_ref, win_ref, qn_ref, wuqT_ref, kvn_ref,
                   wuk_ref, wuvT_ref, gb_ref, u_ref, qT_ref, k_ref, vT_ref):
    ts = x_ref.shape[1]
    h = _rms(x_ref[0], g_ref[0, 0:1, :]).astype(BF16)
    proj = _dot(h, win_ref[...])
    o = 0
    gb_ref[0] = proj[:, o:o + SC_WIDTH].astype(BF16)
    o += SC_WIDTH
    u_ref[0] = (proj[:, o:o + SC_WIDTH] * proj[:, o + SC_WIDTH:o + 2 * SC_WIDTH]).astype(BF16)
    o += 2 * SC_WIDTH
    qn = _rms(proj[:, o:o + MLA_Q_RANK], qn_ref[...]).astype(BF16)
    o += MLA_Q_RANK
    cn = _rms(proj[:, o:o + MLA_KV_RANK], kvn_ref[...]).astype(BF16)
    o += MLA_KV_RANK
    kr = proj[:, o:o + HEAD_SLOT]

    ang = invf_ref[...] * pos_ref[0].astype(F32)
    c = jnp.cos(ang)
    s = jnp.sin(ang)

    krT = kr.T
    r1, r2 = _rope_rows(krT[ROPE_LO:ROPE_LO + HALF], krT[ROPE_LO + HALF:ROPE_LO + 2 * HALF], c, s)
    kpeT = jnp.concatenate(
        [jnp.zeros((ROPE_LO, ts), F32), r1, r2,
         jnp.zeros((HEAD_SLOT - ROPE_LO - 2 * HALF, ts), F32)], axis=0)
    kpe = kpeT.T

    knope = _dot(cn, wuk_ref[...])
    for hh in range(MLA_HEADS):
        sl = slice(hh * HEAD_SLOT, (hh + 1) * HEAD_SLOT)
        k_ref[0, :, sl] = (knope[:, sl] + kpe).astype(BF16)

    vT_ref[0, 0] = _dot_nt(wuvT_ref[...], cn).astype(BF16)

    qT = _dot_nt(wuqT_ref[...], qn)
    for hh in range(MLA_HEADS):
        b0 = hh * HEAD_SLOT
        r1, r2 = _rope_rows(qT[b0 + ROPE_LO:b0 + ROPE_LO + HALF],
                            qT[b0 + ROPE_LO + HALF:b0 + ROPE_LO + 2 * HALF], c, s)
        blk = jnp.concatenate([qT[b0:b0 + ROPE_LO], r1, r2,
                               qT[b0 + ROPE_LO + 2 * HALF:b0 + HEAD_SLOT]], axis=0)
        qT_ref[0, b0:b0 + HEAD_SLOT, :] = (blk * Q_SCALE).astype(BF16)


def _attn_kernel(qT_ref, qTn_ref, k_ref, vT_ref, o_ref, m_ref, acc_ref, s_ref, cm_ref):
    n_chunks, tk = vT_ref.shape[1], vT_ref.shape[3]
    n_heads = m_ref.shape[0]
    m_ref[...] = jnp.full(m_ref.shape, -1e30, F32)
    acc_ref[...] = jnp.zeros(acc_ref.shape, F32)
    ones = jnp.ones((HALO, tk), BF16)

    def scores(ci, a, slot, q_ref=qT_ref):
        start = ci * tk if isinstance(ci, int) else pl.multiple_of(ci * tk, tk)
        kc = k_ref[0, pl.ds(start, tk), a * HEAD_SLOT:(a + 1) * HEAD_SLOT]
        s = _dot(kc, q_ref[0, a * HEAD_SLOT:(a + 1) * HEAD_SLOT, :])
        s_ref[slot, a] = s
        cm_ref[slot, a] = jnp.max(s, axis=0, keepdims=True)

    def consume(ci, a, slot):
        m_prev = m_ref[a]
        m_new = jnp.maximum(m_prev, cm_ref[slot, a])
        alpha = jnp.exp2(m_prev - m_new)
        p = jnp.exp2(s_ref[slot, a] - m_new).astype(BF16)
        vt = jnp.concatenate([vT_ref[0, ci, a * MLA_V:(a + 1) * MLA_V, :], ones], axis=0)
        acc_ref[a] = alpha * acc_ref[a] + _dot(vt, p)
        m_ref[a] = m_new

    @pl.when(pl.program_id(2) == 0)
    def _():
        for a in range(n_heads):
            scores(0, a, 0)

    def group(base, size, last):
        for j in range(size):
            ci = base + j
            for a in range(n_heads):
                if last and j == size - 1:
                    scores(0, a, (j + 1) % 2, qTn_ref)
                else:
                    scores(ci + 1, a, (j + 1) % 2)
                consume(ci, a, j % 2)

    def body(g, carry):
        group(g * KV_UNROLL, KV_UNROLL, False)
        return carry

    n_trips = (n_chunks - 2) // KV_UNROLL
    n_tail = n_chunks - n_trips * KV_UNROLL
    assert KV_UNROLL % 2 == 0 and n_tail % 2 == 0
    lax.fori_loop(0, n_trips, body, 0)
    group(n_trips * KV_UNROLL, n_tail, True)
    outs = []
    for a in range(n_heads):
        acc = acc_ref[a]
        outs.append(acc[:MLA_V] / acc[MLA_V:MLA_V + 1])
    o_ref[0] = jnp.concatenate(outs, axis=0).T.astype(BF16)


def _mlp(h2, w1_ref, w2_ref):
    acc = None
    for c0 in range(0, w1_ref.shape[1], FF_CHUNK):
        hid = _dot(h2, w1_ref[:, c0:c0 + FF_CHUNK])
        a = jnp.square(jnp.maximum(hid, 0.0)).astype(BF16)
        part = _dot(a, w2_ref[c0:c0 + FF_CHUNK, :])
        acc = part if acc is None else acc + part
    return acc


def _sandwich_tail(x, m, g_ref, w1_ref, w2_ref):
    x1 = x + _rms(m, g_ref[0, 1:2, :])
    h2 = _rms(x1, g_ref[0, 2:3, :]).astype(BF16)
    return x1 + _rms(_mlp(h2, w1_ref, w2_ref), g_ref[0, 3:4, :])


def _tail0_kernel(x_ref, gb_ref, u_ref, up_ref, un_ref, yb_ref, sck_ref, g_ref, gn_ref,
                  woa_ref, wob_ref, w1_ref, w2_ref, wpw1_ref, bpw1_ref, x2_ref, u1_ref):
    i = pl.program_id(1)
    tm = x_ref.shape[1]
    u = u_ref[0].astype(F32)
    prev_row = jnp.where(i > 0, up_ref[0].astype(F32)[HALO - 1:HALO, :], 0.0)
    next_row = jnp.where(i < pl.num_programs(1) - 1, un_ref[0].astype(F32)[0:1, :], 0.0)
    row = lax.broadcasted_iota(jnp.int32, u.shape, 0)
    u_m1 = jnp.where(row == 0, prev_row, pltpu.roll(u, 1, 0))
    u_p1 = jnp.where(row == tm - 1, next_row, pltpu.roll(u, tm - 1, 0))
    conv = sck_ref[0:1, :] * u_m1 + sck_ref[1:2, :] * u + sck_ref[2:3, :] * u_p1
    ya = (gb_ref[0].astype(F32) * conv).astype(BF16)
    m = _dot(ya, woa_ref[...]) + _dot(yb_ref[0], wob_ref[...])
    x2 = _sandwich_tail(x_ref[0], m, g_ref, w1_ref, w2_ref)
    x2_ref[0] = x2
    hn = _rms(x2, gn_ref[0, 0:1, :]).astype(BF16)
    pu = _dot(hn, wpw1_ref[...]) + bpw1_ref[...]
    cw = pu.shape[1] // 2
    u1_ref[0] = pu[:, :cw] * jax.nn.sigmoid(pu[:, cw:])


def _tail1_kernel(x_ref, u_ref, up_ref, un_ref, wdw_ref, bdw_ref, lng_ref, lnb_ref, wpw2_ref,
                  bpw2_ref, g_ref, w1_ref, w2_ref, o_ref, ext_ref, sh_ref, conv_ref):
    i = pl.program_id(1)
    tm, d = u_ref.shape[1], u_ref.shape[2]
    taps = wdw_ref.shape[0] // SUBLANES
    pad = taps // 2
    ext_ref[0:HALO, :] = jnp.where(i > 0, up_ref[0], 0.0)
    ext_ref[HALO:HALO + tm, :] = u_ref[0]
    ext_ref[HALO + tm:HALO + tm + HALO, :] = jnp.where(i < pl.num_programs(1) - 1, un_ref[0], 0.0)
    n_sh = tm + SUBLANES * ((HALO - pad + taps - 1) // SUBLANES)
    for c0 in range(0, d, CONV_CB):
        cols = slice(c0, c0 + CONV_CB)
        for rho in range(1, SUBLANES):
            sh_ref[rho - 1, 0:n_sh, :] = ext_ref[rho:rho + n_sh, cols]
        for r0 in range(0, tm, CONV_RB):
            acc = jnp.zeros((CONV_RB // SUBLANES, SUBLANES, CONV_CB), F32)
            for k in range(taps):
                off = HALO - pad + k
                rho, q = off % SUBLANES, off - off % SUBLANES
                if rho == 0:
                    src = ext_ref[r0 + q:r0 + q + CONV_RB, cols]
                else:
                    src = sh_ref[rho - 1, r0 + q:r0 + q + CONV_RB, :]
                w8 = wdw_ref[k * SUBLANES:(k + 1) * SUBLANES, cols]
                acc = acc + w8[None] * src.reshape(CONV_RB // SUBLANES, SUBLANES, CONV_CB)
            conv_ref[r0:r0 + CONV_RB, cols] = acc.reshape(CONV_RB, CONV_CB) + bdw_ref[:, cols]
    v = conv_ref[...]
    mu = jnp.mean(v, axis=-1, keepdims=True)
    vc = v - mu
    var = jnp.mean(vc * vc, axis=-1, keepdims=True)
    y = vc * lax.rsqrt(var + EPS) * lng_ref[...] + lnb_ref[...]
    y = (y * jax.nn.sigmoid(y)).astype(BF16)
    m = _dot(y, wpw2_ref[...]) + bpw2_ref[...]
    o_ref[0] = _sandwich_tail(x_ref[0], m, g_ref, w1_ref, w2_ref)


def _halo_specs(tm, width, n_rows):
    per = tm // HALO
    last = n_rows // HALO - 1
    prev = pl.BlockSpec((1, HALO, width), lambda b, i: (b, jnp.maximum(i * per - 1, 0), 0))
    nxt = pl.BlockSpec((1, HALO, width), lambda b, i: (b, jnp.minimum((i + 1) * per, last), 0))
    return prev, nxt


def _params(sem, flags=None):
    return pltpu.CompilerParams(dimension_semantics=sem, vmem_limit_bytes=VMEM_LIMIT, flags=flags)


def kernel(x, positions, sandwich_gains, even_w_in, even_sc_kernel, even_q_norm, even_w_uq, even_kv_norm, even_w_ukv, even_w_out, odd_w_pw1, odd_b_pw1, odd_w_dw, odd_b_dw, odd_ln_g, odd_ln_b, odd_w_pw2, odd_b_pw2, mlp_w1, mlp_w2):
    B, S, D = x.shape
    H = MLA_HEADS
    d_ff = mlp_w1.shape[2]
    n_in = S // TS_IN
    assert S % TS_IN == 0 and S % TQ == 0 and S % TM0 == 0 and S % TM1 == 0

    w_in = even_w_in[0]
    lat_end = 3 * SC_WIDTH + MLA_Q_RANK + MLA_KV_RANK
    win_p = jnp.concatenate(
        [w_in[:, :lat_end], jnp.zeros((D, ROPE_LO), F32), w_in[:, lat_end:],
         jnp.zeros((D, HEAD_SLOT - ROPE_LO - MLA_ROPE), F32)], axis=1).astype(BF16)
    w_uq = even_w_uq[0].reshape(MLA_Q_RANK, H, MLA_NOPE + MLA_ROPE)
    wuqT = jnp.pad(w_uq, ((0, 0), (0, 0), (0, HEAD_SLOT - MLA_NOPE - MLA_ROPE))
                   ).reshape(MLA_Q_RANK, H * HEAD_SLOT).T.astype(BF16)
    w_ukv = even_w_ukv[0].reshape(MLA_KV_RANK, H, MLA_NOPE + MLA_V)
    wuk = jnp.pad(w_ukv[:, :, :MLA_NOPE], ((0, 0), (0, 0), (0, HEAD_SLOT - MLA_NOPE))
                  ).reshape(MLA_KV_RANK, H * HEAD_SLOT).astype(BF16)
    wuvT = w_ukv[:, :, MLA_NOPE:].reshape(MLA_KV_RANK, H * MLA_V).T.astype(BF16)
    woa = even_w_out[0, :SC_WIDTH].astype(BF16)
    wob = even_w_out[0, SC_WIDTH:].astype(BF16)
    w1 = mlp_w1.astype(BF16)
    w2 = mlp_w2.astype(BF16)
    wpw1 = odd_w_pw1[0].astype(BF16)
    wpw2 = odd_w_pw2[0].astype(BF16)
    inv_freq = 1.0 / (ROPE_BASE ** (jnp.arange(HALF, dtype=F32) / HALF))
    invf = jnp.broadcast_to(inv_freq[:, None], (HALF, TS_IN))
    pos3 = positions.reshape(B, 1, S)
    g0 = sandwich_gains[0:1]
    g1 = sandwich_gains[1:2]

    seq_tile = lambda w: pl.BlockSpec((1, TS_IN, w), lambda b, i: (b, i, 0))
    gb, u, qT, k, vT = pl.pallas_call(
        _inproj_kernel,
        grid=(B, n_in),
        in_specs=[
            seq_tile(D),
            pl.BlockSpec((1, 1, TS_IN), lambda b, i: (b, 0, i)),
            _const_spec((HALF, TS_IN)),
            _const_spec((1, 4, D)),
            _const_spec(win_p.shape),
            _const_spec((1, MLA_Q_RANK)),
            _const_spec(wuqT.shape),
            _const_spec((1, MLA_KV_RANK)),
            _const_spec(wuk.shape),
            _const_spec(wuvT.shape),
        ],
        out_specs=[
            seq_tile(SC_WIDTH),
            seq_tile(SC_WIDTH),
            pl.BlockSpec((1, H * HEAD_SLOT, TS_IN), lambda b, i: (b, 0, i)),
            seq_tile(H * HEAD_SLOT),
            pl.BlockSpec((1, 1, H * MLA_V, TS_IN), lambda b, i: (b, i, 0, 0)),
        ],
        out_shape=[
            jax.ShapeDtypeStruct((B, S, SC_WIDTH), BF16),
            jax.ShapeDtypeStruct((B, S, SC_WIDTH), BF16),
            jax.ShapeDtypeStruct((B, H * HEAD_SLOT, S), BF16),
            jax.ShapeDtypeStruct((B, S, H * HEAD_SLOT), BF16),
            jax.ShapeDtypeStruct((B, n_in, H * MLA_V, TS_IN), BF16),
        ],
        compiler_params=_params(("parallel", "parallel")),
        name="inproj",
    )(x, pos3, invf, g0, win_p, even_q_norm, wuqT, even_kv_norm, wuk, wuvT)

    nh = ATTN_HEADS
    yb = pl.pallas_call(
        _attn_kernel,
        grid=(B, H // nh, S // TQ),
        in_specs=[
            pl.BlockSpec((1, nh * HEAD_SLOT, TQ), lambda b, j, i: (b, j, i)),
            pl.BlockSpec((1, nh * HEAD_SLOT, TQ), lambda b, j, i: (b, j, jnp.minimum(i + 1, S // TQ - 1))),
            pl.BlockSpec((1, S, nh * HEAD_SLOT), lambda b, j, i: (b, 0, j)),
            pl.BlockSpec((1, n_in, nh * MLA_V, TS_IN), lambda b, j, i: (b, 0, j, 0)),
        ],
        out_specs=pl.BlockSpec((1, TQ, nh * MLA_V), lambda b, j, i: (b, i, j)),
        out_shape=jax.ShapeDtypeStruct((B, S, H * MLA_V), BF16),
        scratch_shapes=[
            pltpu.VMEM((nh, 1, TQ), F32),
            pltpu.VMEM((nh, MLA_V + HALO, TQ), F32),
            pltpu.VMEM((2, nh, TS_IN, TQ), F32),
            pltpu.VMEM((2, nh, 1, TQ), F32),
        ],
        compiler_params=_params(("arbitrary", "arbitrary", "arbitrary")),
        name="attn",
    )(qT, qT, k, vT)

    row0 = lambda w: pl.BlockSpec((1, TM0, w), lambda b, i: (b, i, 0))
    up0, un0 = _halo_specs(TM0, SC_WIDTH, S)
    x2, u1 = pl.pallas_call(
        _tail0_kernel,
        grid=(B, S // TM0),
        in_specs=[
            row0(D), row0(SC_WIDTH), row0(SC_WIDTH), up0, un0, row0(H * MLA_V),
            _const_spec((3, SC_WIDTH)),
            _const_spec((1, 4, D)),
            _const_spec((1, 4, D)),
            _const_spec(woa.shape), _const_spec(wob.shape),
            pl.BlockSpec((None, D, d_ff), lambda b, i: (0, 0, 0), pipeline_mode=pl.Buffered(1)),
            pl.BlockSpec((None, d_ff, D), lambda b, i: (0, 0, 0), pipeline_mode=pl.Buffered(1)),
            _const_spec(wpw1.shape),
            _const_spec((1, 2 * D)),
        ],
        out_specs=[row0(D), row0(D)],
        out_shape=[jax.ShapeDtypeStruct((B, S, D), F32), jax.ShapeDtypeStruct((B, S, D), F32)],
        compiler_params=_params(("parallel", "parallel")),
        name="tail0",
    )(x, gb, u, u, u, yb, even_sc_kernel[0], g0, g1, woa, wob, w1, w2, wpw1, odd_b_pw1)

    row1 = lambda w: pl.BlockSpec((1, TM1, w), lambda b, i: (b, i, 0))
    up1, un1 = _halo_specs(TM1, D, S)
    wdw8 = jnp.repeat(odd_w_dw[0], SUBLANES, axis=0)
    out = pl.pallas_call(
        _tail1_kernel,
        grid=(B, S // TM1),
        in_specs=[
            row1(D), row1(D), up1, un1,
            _const_spec(wdw8.shape),
            _const_spec((1, D)), _const_spec((1, D)), _const_spec((1, D)),
            _const_spec(wpw2.shape),
            _const_spec((1, D)),
            _const_spec((1, 4, D)),
            pl.BlockSpec((None, D, d_ff), lambda b, i: (1, 0, 0), pipeline_mode=pl.Buffered(1)),
            pl.BlockSpec((None, d_ff, D), lambda b, i: (1, 0, 0), pipeline_mode=pl.Buffered(1)),
        ],
        out_specs=row1(D),
        out_shape=jax.ShapeDtypeStruct((B, S, D), F32),
        scratch_shapes=[
            pltpu.VMEM((TM1 + 2 * HALO, D), F32),
            pltpu.VMEM((SUBLANES - 1, TM1 + 2 * HALO, CONV_CB), F32),
            pltpu.VMEM((TM1, D), F32),
        ],
        compiler_params=_params(("parallel", "parallel")),
        name="tail1",
    )(x2, u1, u1, u1, wdw8, odd_b_dw, odd_ln_g, odd_ln_b, wpw2, odd_b_pw2, g1, w1, w2)
    return out
```

```python
import functools
import math

import jax
import jax.numpy as jnp
from jax import lax
from jax.experimental import pallas as pl
from jax.experimental.pallas import tpu as pltpu

F32 = jnp.float32
BF16 = jnp.bfloat16

MLA_HEADS = 8
MLA_NOPE = 64
MLA_ROPE = 32
MLA_V = 64
MLA_Q_RANK = 384
MLA_KV_RANK = 256
SC_WIDTH = 512
ROPE_BASE = 10000.0
EPS = 1e-6

LANES = 128
SUBLANES = 8
HEAD_SLOT = LANES
ROPE_LO = MLA_NOPE
HALF = MLA_ROPE // 2
HALO = 16
VMEM_LIMIT = 56 * 1024 * 1024

TS_IN = 512
TQ = 512
KV_UNROLL = 6
ATTN_HEADS = 8
TM0 = 512
TM1 = 512
CONV_RB = 64
CONV_CB = 256
FF_CHUNK = 1024

Q_SCALE = float((MLA_NOPE + MLA_ROPE) ** -0.5 * math.log2(math.e))


def _rms(x, g):
    return x * lax.rsqrt(jnp.mean(x * x, axis=-1, keepdims=True) + EPS) * g


def _const_spec(shape):
    nd = len(shape)
    return pl.BlockSpec(shape, lambda *_: (0,) * nd, pipeline_mode=pl.Buffered(1))


def _dot(a, b):
    return jnp.dot(a, b, preferred_element_type=F32)


def _dot_nt(a, b):
    return lax.dot_general(a, b, (((1,), (1,)), ((), ())), preferred_element_type=F32)


def _rope_rows(x1, x2, c, s):
    return x1 * c - x2 * s, x2 * c + x1 * s


def _inproj_kernel(x_ref, pos_ref, invf_ref, g_ref, win_ref, qn_ref, wuqT_ref, kvn_ref,
                   wuk_ref, wuvT_ref, gb_ref, u_ref, qT_ref, k_ref, vT_ref):
    ts = x_ref.shape[1]
    h = _rms(x_ref[0], g_ref[0, 0:1, :]).astype(BF16)
    proj = _dot(h, win_ref[...])
    o = 0
    gb_ref[0] = proj[:, o:o + SC_WIDTH].astype(BF16)
    o += SC_WIDTH
    u_ref[0] = (proj[:, o:o + SC_WIDTH] * proj[:, o + SC_WIDTH:o + 2 * SC_WIDTH]).astype(BF16)
    o += 2 * SC_WIDTH
    qn = _rms(proj[:, o:o + MLA_Q_RANK], qn_ref[...]).astype(BF16)
    o += MLA_Q_RANK
    cn = _rms(proj[:, o:o + MLA_KV_RANK], kvn_ref[...]).astype(BF16)
    o += MLA_KV_RANK
    kr = proj[:, o:o + HEAD_SLOT]

    ang = invf_ref[...] * pos_ref[0].astype(F32)
    c = jnp.cos(ang)
    s = jnp.sin(ang)

    krT = kr.T
    r1, r2 = _rope_rows(krT[ROPE_LO:ROPE_LO + HALF], krT[ROPE_LO + HALF:ROPE_LO + 2 * HALF], c, s)
    kpeT = jnp.concatenate(
        [jnp.zeros((ROPE_LO, ts), F32), r1, r2,
         jnp.zeros((HEAD_SLOT - ROPE_LO - 2 * HALF, ts), F32)], axis=0)
    kpe = kpeT.T

    knope = _dot(cn, wuk_ref[...])
    for hh in range(MLA_HEADS):
        sl = slice(hh * HEAD_SLOT, (hh + 1) * HEAD_SLOT)
        k_ref[0, :, sl] = (knope[:, sl] + kpe).astype(BF16)

    vT_ref[0, 0] = _dot_nt(wuvT_ref[...], cn).astype(BF16)

    qT = _dot_nt(wuqT_ref[...], qn)
    for hh in range(MLA_HEADS):
        b0 = hh * HEAD_SLOT
        r1, r2 = _rope_rows(qT[b0 + ROPE_LO:b0 + ROPE_LO + HALF],
                            qT[b0 + ROPE_LO + HALF:b0 + ROPE_LO + 2 * HALF], c, s)
        blk = jnp.concatenate([qT[b0:b0 + ROPE_LO], r1, r2,
                               qT[b0 + ROPE_LO + 2 * HALF:b0 + HEAD_SLOT]], axis=0)
        qT_ref[0, b0:b0 + HEAD_SLOT, :] = (blk * Q_SCALE).astype(BF16)


def _attn_kernel(qT_ref, qTn_ref, k_ref, vT_ref, o_ref, m_ref, acc_ref, s_ref, cm_ref):
    n_chunks, tk = vT_ref.shape[1], vT_ref.shape[3]
    n_heads = m_ref.shape[0]
    m_ref[...] = jnp.full(m_ref.shape, -1e30, F32)
    acc_ref[...] = jnp.zeros(acc_ref.shape, F32)
    ones = jnp.ones((HALO, tk), BF16)

    def scores(ci, a, slot, q_ref=qT_ref):
        start = ci * tk if isinstance(ci, int) else pl.multiple_of(ci * tk, tk)
        kc = k_ref[0, pl.ds(start, tk), a * HEAD_SLOT:(a + 1) * HEAD_SLOT]
        s = _dot(kc, q_ref[0, a * HEAD_SLOT:(a + 1) * HEAD_SLOT, :])
        s_ref[slot, a] = s
        cm_ref[slot, a] = jnp.max(s, axis=0, keepdims=True)

    def consume(ci, a, slot):
        m_prev = m_ref[a]
        m_new = jnp.maximum(m_prev, cm_ref[slot, a])
        alpha = jnp.exp2(m_prev - m_new)
        p = jnp.exp2(s_ref[slot, a] - m_new).astype(BF16)
        vt = jnp.concatenate([vT_ref[0, ci, a * MLA_V:(a + 1) * MLA_V, :], ones], axis=0)
        acc_ref[a] = alpha * acc_ref[a] + _dot(vt, p)
        m_ref[a] = m_new

    @pl.when(pl.program_id(2) == 0)
    def _():
        for a in range(n_heads):
            scores(0, a, 0)

    def group(base, size, last):
        for j in range(size):
            ci = base + j
            for a in range(n_heads):
                if last and j == size - 1:
                    scores(0, a, (j + 1) % 2, qTn_ref)
                else:
                    scores(ci + 1, a, (j + 1) % 2)
                consume(ci, a, j % 2)

    def body(g, carry):
        group(g * KV_UNROLL, KV_UNROLL, False)
        return carry

    n_trips = (n_chunks - 2) // KV_UNROLL
    n_tail = n_chunks - n_trips * KV_UNROLL
    assert KV_UNROLL % 2 == 0 and n_tail % 2 == 0
    lax.fori_loop(0, n_trips, body, 0)
    group(n_trips * KV_UNROLL, n_tail, True)
    outs = []
    for a in range(n_heads):
        acc = acc_ref[a]
        outs.append(acc[:MLA_V] / acc[MLA_V:MLA_V + 1])
    o_ref[0] = jnp.concatenate(outs, axis=0).T.astype(BF16)


def _mlp(h2, w1_ref, w2_ref):
    acc = None
    for c0 in range(0, w1_ref.shape[1], FF_CHUNK):
        hid = _dot(h2, w1_ref[:, c0:c0 + FF_CHUNK])
        a = jnp.square(jnp.maximum(hid, 0.0)).astype(BF16)
        part = _dot(a, w2_ref[c0:c0 + FF_CHUNK, :])
        acc = part if acc is None else acc + part
    return acc


def _sandwich_tail(x, m, g_ref, w1_ref, w2_ref):
    x1 = x + _rms(m, g_ref[0, 1:2, :])
    h2 = _rms(x1, g_ref[0, 2:3, :]).astype(BF16)
    return x1 + _rms(_mlp(h2, w1_ref, w2_ref), g_ref[0, 3:4, :])


def _tail0_kernel(x_ref, gb_ref, u_ref, up_ref, un_ref, yb_ref, sck_ref, g_ref, gn_ref,
                  woa_ref, wob_ref, w1_ref, w2_ref, wpw1_ref, bpw1_ref, x2_ref, u1_ref):
    i = pl.program_id(1)
    tm = x_ref.shape[1]
    u = u_ref[0].astype(F32)
    prev_row = jnp.where(i > 0, up_ref[0].astype(F32)[HALO - 1:HALO, :], 0.0)
    next_row = jnp.where(i < pl.num_programs(1) - 1, un_ref[0].astype(F32)[0:1, :], 0.0)
    row = lax.broadcasted_iota(jnp.int32, u.shape, 0)
    u_m1 = jnp.where(row == 0, prev_row, pltpu.roll(u, 1, 0))
    u_p1 = jnp.where(row == tm - 1, next_row, pltpu.roll(u, tm - 1, 0))
    conv = sck_ref[0:1, :] * u_m1 + sck_ref[1:2, :] * u + sck_ref[2:3, :] * u_p1
    ya = (gb_ref[0].astype(F32) * conv).astype(BF16)
    m = _dot(ya, woa_ref[...]) + _dot(yb_ref[0], wob_ref[...])
    x2 = _sandwich_tail(x_ref[0], m, g_ref, w1_ref, w2_ref)
    x2_ref[0] = x2
    hn = _rms(x2, gn_ref[0, 0:1, :]).astype(BF16)
    pu = _dot(hn, wpw1_ref[...]) + bpw1_ref[...]
    cw = pu.shape[1] // 2
    u1_ref[0] = pu[:, :cw] * jax.nn.sigmoid(pu[:, cw:])


def _tail1_kernel(x_ref, u_ref, up_ref, un_ref, wdw_ref, bdw_ref, lng_ref, lnb_ref, wpw2_ref,
                  bpw2_ref, g_ref, w1_ref, w2_ref, o_ref, ext_ref, sh_ref, conv_ref):
    i = pl.program_id(1)
    tm, d = u_ref.shape[1], u_ref.shape[2]
    taps = wdw_ref.shape[0] // SUBLANES
    pad = taps // 2
    ext_ref[0:HALO, :] = jnp.where(i > 0, up_ref[0], 0.0)
    ext_ref[HALO:HALO + tm, :] = u_ref[0]
    ext_ref[HALO + tm:HALO + tm + HALO, :] = jnp.where(i < pl.num_programs(1) - 1, un_ref[0], 0.0)
    n_sh = tm + SUBLANES * ((HALO - pad + taps - 1) // SUBLANES)
    for c0 in range(0, d, CONV_CB):
        cols = slice(c0, c0 + CONV_CB)
        for rho in range(1, SUBLANES):
            sh_ref[rho - 1, 0:n_sh, :] = ext_ref[rho:rho + n_sh, cols]
        for r0 in range(0, tm, CONV_RB):
            acc = jnp.zeros((CONV_RB // SUBLANES, SUBLANES, CONV_CB), F32)
            for k in range(taps):
                off = HALO - pad + k
                rho, q = off % SUBLANES, off - off % SUBLANES
                if rho == 0:
                    src = ext_ref[r0 + q:r0 + q + CONV_RB, cols]
                else:
                    src = sh_ref[rho - 1, r0 + q:r0 + q + CONV_RB, :]
                w8 = wdw_ref[k * SUBLANES:(k + 1) * SUBLANES, cols]
                acc = acc + w8[None] * src.reshape(CONV_RB // SUBLANES, SUBLANES, CONV_CB)
            conv_ref[r0:r0 + CONV_RB, cols] = acc.reshape(CONV_RB, CONV_CB) + bdw_ref[:, cols]
    v = conv_ref[...]
    mu = jnp.mean(v, axis=-1, keepdims=True)
    vc = v - mu
    var = jnp.mean(vc * vc, axis=-1, keepdims=True)
    y = vc * lax.rsqrt(var + EPS) * lng_ref[...] + lnb_ref[...]
    y = (y * jax.nn.sigmoid(y)).astype(BF16)
    m = _dot(y, wpw2_ref[...]) + bpw2_ref[...]
    o_ref[0] = _sandwich_tail(x_ref[0], m, g_ref, w1_ref, w2_ref)


def _halo_specs(tm, width, n_rows):
    per = tm // HALO
    last = n_rows // HALO - 1
    prev = pl.BlockSpec((1, HALO, width), lambda b, i: (b, jnp.maximum(i * per - 1, 0), 0))
    nxt = pl.BlockSpec((1, HALO, width), lambda b, i: (b, jnp.minimum((i + 1) * per, last), 0))
    return prev, nxt


def _params(sem, flags=None):
    return pltpu.CompilerParams(dimension_semantics=sem, vmem_limit_bytes=VMEM_LIMIT, flags=flags)


def kernel(x, positions, sandwich_gains, even_w_in, even_sc_kernel, even_q_norm, even_w_uq, even_kv_norm, even_w_ukv, even_w_out, odd_w_pw1, odd_b_pw1, odd_w_dw, odd_b_dw, odd_ln_g, odd_ln_b, odd_w_pw2, odd_b_pw2, mlp_w1, mlp_w2):
    B, S, D = x.shape
    H = MLA_HEADS
    d_ff = mlp_w1.shape[2]
    n_in = S // TS_IN
    assert S % TS_IN == 0 and S % TQ == 0 and S % TM0 == 0 and S % TM1 == 0

    w_in = even_w_in[0]
    lat_end = 3 * SC_WIDTH + MLA_Q_RANK + MLA_KV_RANK
    win_p = jnp.concatenate(
        [w_in[:, :lat_end], jnp.zeros((D, ROPE_LO), F32), w_in[:, lat_end:],
         jnp.zeros((D, HEAD_SLOT - ROPE_LO - MLA_ROPE), F32)], axis=1).astype(BF16)
    w_uq = even_w_uq[0].reshape(MLA_Q_RANK, H, MLA_NOPE + MLA_ROPE)
    wuqT = jnp.pad(w_uq, ((0, 0), (0, 0), (0, HEAD_SLOT - MLA_NOPE - MLA_ROPE))
                   ).reshape(MLA_Q_RANK, H * HEAD_SLOT).T.astype(BF16)
    w_ukv = even_w_ukv[0].reshape(MLA_KV_RANK, H, MLA_NOPE + MLA_V)
    wuk = jnp.pad(w_ukv[:, :, :MLA_NOPE], ((0, 0), (0, 0), (0, HEAD_SLOT - MLA_NOPE))
                  ).reshape(MLA_KV_RANK, H * HEAD_SLOT).astype(BF16)
    wuvT = w_ukv[:, :, MLA_NOPE:].reshape(MLA_KV_RANK, H * MLA_V).T.astype(BF16)
    woa = even_w_out[0, :SC_WIDTH].astype(BF16)
    wob = even_w_out[0, SC_WIDTH:].astype(BF16)
    w1 = mlp_w1.astype(BF16)
    w2 = mlp_w2.astype(BF16)
    wpw1 = odd_w_pw1[0].astype(BF16)
    wpw2 = odd_w_pw2[0].astype(BF16)
    inv_freq = 1.0 / (ROPE_BASE ** (jnp.arange(HALF, dtype=F32) / HALF))
    invf = jnp.broadcast_to(inv_freq[:, None], (HALF, TS_IN))
    pos3 = positions.reshape(B, 1, S)
    g0 = sandwich_gains[0:1]
    g1 = sandwich_gains[1:2]

    seq_tile = lambda w: pl.BlockSpec((1, TS_IN, w), lambda b, i: (b, i, 0))
    gb, u, qT, k, vT = pl.pallas_call(
        _inproj_kernel,
        grid=(B, n_in),
        in_specs=[
            seq_tile(D),
            pl.BlockSpec((1, 1, TS_IN), lambda b, i: (b, 0, i)),
            _const_spec((HALF, TS_IN)),
            _const_spec((1, 4, D)),
            _const_spec(win_p.shape),
            _const_spec((1, MLA_Q_RANK)),
            _const_spec(wuqT.shape),
            _const_spec((1, MLA_KV_RANK)),
            _const_spec(wuk.shape),
            _const_spec(wuvT.shape),
        ],
        out_specs=[
            seq_tile(SC_WIDTH),
            seq_tile(SC_WIDTH),
            pl.BlockSpec((1, H * HEAD_SLOT, TS_IN), lambda b, i: (b, 0, i)),
            seq_tile(H * HEAD_SLOT),
            pl.BlockSpec((1, 1, H * MLA_V, TS_IN), lambda b, i: (b, i, 0, 0)),
        ],
        out_shape=[
            jax.ShapeDtypeStruct((B, S, SC_WIDTH), BF16),
            jax.ShapeDtypeStruct((B, S, SC_WIDTH), BF16),
            jax.ShapeDtypeStruct((B, H * HEAD_SLOT, S), BF16),
            jax.ShapeDtypeStruct((B, S, H * HEAD_SLOT), BF16),
            jax.ShapeDtypeStruct((B, n_in, H * MLA_V, TS_IN), BF16),
        ],
        compiler_params=_params(("parallel", "parallel")),
        name="inproj",
    )(x, pos3, invf, g0, win_p, even_q_norm, wuqT, even_kv_norm, wuk, wuvT)

    nh = ATTN_HEADS
    yb = pl.pallas_call(
        _attn_kernel,
        grid=(B, H // nh, S // TQ),
        in_specs=[
            pl.BlockSpec((1, nh * HEAD_SLOT, TQ), lambda b, j, i: (b, j, i)),
            pl.BlockSpec((1, nh * HEAD_SLOT, TQ), lambda b, j, i: (b, j, jnp.minimum(i + 1, S // TQ - 1))),
            pl.BlockSpec((1, S, nh * HEAD_SLOT), lambda b, j, i: (b, 0, j), pipeline_mode=pl.Buffered(1)),
            pl.BlockSpec((1, n_in, nh * MLA_V, TS_IN), lambda b, j, i: (b, 0, j, 0), pipeline_mode=pl.Buffered(1)),
        ],
        out_specs=pl.BlockSpec((1, TQ, nh * MLA_V), lambda b, j, i: (b, i, j)),
        out_shape=jax.ShapeDtypeStruct((B, S, H * MLA_V), BF16),
        scratch_shapes=[
            pltpu.VMEM((nh, 1, TQ), F32),
            pltpu.VMEM((nh, MLA_V + HALO, TQ), F32),
            pltpu.VMEM((2, nh, TS_IN, TQ), F32),
            pltpu.VMEM((2, nh, 1, TQ), F32),
        ],
        compiler_params=_params(("arbitrary", "arbitrary", "arbitrary")),
        name="attn",
    )(qT, qT, k, vT)

    row0 = lambda w: pl.BlockSpec((1, TM0, w), lambda b, i: (b, i, 0))
    up0, un0 = _halo_specs(TM0, SC_WIDTH, S)
    x2, u1 = pl.pallas_call(
        _tail0_kernel,
        grid=(B, S // TM0),
        in_specs=[
            row0(D), row0(SC_WIDTH), row0(SC_WIDTH), up0, un0, row0(H * MLA_V),
            _const_spec((3, SC_WIDTH)),
            _const_spec((1, 4, D)),
            _const_spec((1, 4, D)),
            _const_spec(woa.shape), _const_spec(wob.shape),
            pl.BlockSpec((None, D, d_ff), lambda b, i: (0, 0, 0), pipeline_mode=pl.Buffered(1)),
            pl.BlockSpec((None, d_ff, D), lambda b, i: (0, 0, 0), pipeline_mode=pl.Buffered(1)),
            _const_spec(wpw1.shape),
            _const_spec((1, 2 * D)),
        ],
        out_specs=[row0(D), row0(D)],
        out_shape=[jax.ShapeDtypeStruct((B, S, D), F32), jax.ShapeDtypeStruct((B, S, D), F32)],
        compiler_params=_params(("parallel", "parallel")),
        name="tail0",
    )(x, gb, u, u, u, yb, even_sc_kernel[0], g0, g1, woa, wob, w1, w2, wpw1, odd_b_pw1)

    row1 = lambda w: pl.BlockSpec((1, TM1, w), lambda b, i: (b, i, 0))
    up1, un1 = _halo_specs(TM1, D, S)
    wdw8 = jnp.repeat(odd_w_dw[0], SUBLANES, axis=0)
    out = pl.pallas_call(
        _tail1_kernel,
        grid=(B, S // TM1),
        in_specs=[
            row1(D), row1(D), up1, un1,
            _const_spec(wdw8.shape),
            _const_spec((1, D)), _const_spec((1, D)), _const_spec((1, D)),
            _const_spec(wpw2.shape),
            _const_spec((1, D)),
            _const_spec((1, 4, D)),
            pl.BlockSpec((None, D, d_ff), lambda b, i: (1, 0, 0), pipeline_mode=pl.Buffered(1)),
            pl.BlockSpec((None, d_ff, D), lambda b, i: (1, 0, 0), pipeline_mode=pl.Buffered(1)),
        ],
        out_specs=row1(D),
        out_shape=jax.ShapeDtypeStruct((B, S, D), F32),
        scratch_shapes=[
            pltpu.VMEM((TM1 + 2 * HALO, D), F32),
            pltpu.VMEM((SUBLANES - 1, TM1 + 2 * HALO, CONV_CB), F32),
            pltpu.VMEM((TM1, D), F32),
        ],
        compiler_params=_params(("parallel", "parallel")),
        name="tail1",
    )(x2, u1, u1, u1, wdw8, odd_b_dw, odd_ln_g, odd_ln_b, wpw2, odd_b_pw2, g1, w1, w2)
    return out
```
